```python
import math
import jax, jax.numpy as jnp
from jax import lax
import numpy as np

D_MODEL = 2048
BATCH = 4
SEQ = 4096
DEPTH = 2

CTX_LEN = 256
GRID_W = 64
D_MIX = D_MODEL
MLA_HEADS = 8
MLA_NOPE = 128
MLA_ROPE = 64
MLA_V = 128
MLA_W = MLA_HEADS * MLA_V
Q_LORA = 512
KV_LORA = 256
ROPE_THETA = 10000.0
Q_BLOCK = 128
SCALE = (MLA_NOPE + MLA_ROPE) ** -0.5
HY_W = D_MIX - MLA_W
CONV_W = 3
FILT_EMB = 33
FILT_HIDDEN = 64
FILT_TARGET = 1e-2
FILT_FAST_DECAY = 0.3
FILT_SLOW_DECAY = 1.5
EPS = 1e-6
OFF_Q = 0
OFF_KV = OFF_Q + Q_LORA
OFF_KR = OFF_KV + KV_LORA
OFF_GM = OFF_KR + MLA_ROPE
OFF_HY = OFF_GM + MLA_W
OFF_GH = OFF_HY + 3 * HY_W
N_IN = OFF_GH + HY_W

kernel_name = "hymba_mla_hyena_prefix_dit"

F32 = jnp.float32


def rmsnorm(x, g):
    xf = x.astype(F32)
    y = xf * lax.rsqrt(jnp.mean(xf * xf, axis=-1, keepdims=True) + EPS)
    return (y * g.astype(F32)).astype(x.dtype)


def _modulation(cvec, ada_w, ada_b):
    m = jax.nn.silu(cvec) @ ada_w + ada_b
    return jnp.split(m, 3, axis=-1)


def _axial_rope_tables(n):
    rows = n // GRID_W
    row = jnp.repeat(jnp.arange(rows, dtype=F32), GRID_W)
    col = jnp.tile(jnp.arange(GRID_W, dtype=F32), rows)
    nf = MLA_ROPE // 4
    inv = ROPE_THETA ** (-jnp.arange(nf, dtype=F32) / nf)
    ang = jnp.stack([row[:, None] * inv, col[:, None] * inv], axis=1)
    return jnp.cos(ang), jnp.sin(ang)


def apply_axial_rope(x, cos, sin):
    xr = x.astype(F32).reshape(x.shape[:-1] + (2, 2, MLA_ROPE // 4))
    x1, x2 = xr[..., 0, :], xr[..., 1, :]
    out = jnp.stack([x1 * cos - x2 * sin, x2 * cos + x1 * sin], axis=-2)
    return out.reshape(x.shape).astype(x.dtype)


def _mla_q(p_q, lp):
    q = rmsnorm(p_q, lp["q_norm_g"]) @ lp["w_uq"]
    q = q.reshape(p_q.shape[:-1] + (MLA_HEADS, MLA_NOPE + MLA_ROPE))
    return q[..., :MLA_NOPE], q[..., MLA_NOPE:]


def _mla_kv(p_kv, lp):
    c_kv = rmsnorm(p_kv[..., :KV_LORA], lp["kv_norm_g"])
    kv = (c_kv @ lp["w_ukv"]).reshape(p_kv.shape[:-1] + (MLA_HEADS, MLA_NOPE + MLA_V))
    return kv[..., :MLA_NOPE], kv[..., MLA_NOPE:], p_kv[..., KV_LORA:]


def _attend(q_nope, q_rope, k_nope, k_rope, v):
    s = jnp.einsum("bqhd,bkhd->bhqk", q_nope, k_nope, preferred_element_type=F32)
    s = s + jnp.einsum("bqhr,bkr->bhqk", q_rope, k_rope, preferred_element_type=F32)
    p = jax.nn.softmax(s * SCALE, axis=-1)
    return jnp.einsum("bhqk,bkhd->bqhd", p.astype(v.dtype), v)


def _short_conv(u, w, b):
    L = u.shape[1]
    pad = CONV_W // 2
    up = jnp.pad(u, ((0, 0), (pad, pad), (0, 0)))
    out = b
    for j in range(CONV_W):
        out = out + up[:, j:j + L] * w[j]
    return out


def _implicit_filters(L, lp):
    t = jnp.linspace(0.0, 1.0, L, dtype=F32)[:, None]
    bands = (FILT_EMB - 1) // 2
    f = jnp.linspace(1e-4, bands - 1, bands, dtype=F32)[None, :]
    wpos = (2.0 * math.pi) * jnp.arange(L, dtype=F32)[:, None] / L
    z = jnp.concatenate([t, jnp.cos(f * wpos), -jnp.sin(f * wpos)], axis=-1)
    h = jnp.sin(lp["filt_freq"] * (z @ lp["filt_w1"] + lp["filt_b1"]))
    h = jnp.sin(lp["filt_freq"] * (h @ lp["filt_w2"] + lp["filt_b2"]))
    h = (h @ lp["filt_w3"]).astype(F32).reshape(L, 2, HY_W)
    deltas = jnp.linspace(math.log(FILT_TARGET) / FILT_FAST_DECAY,
                          math.log(FILT_TARGET) / FILT_SLOW_DECAY, HY_W, dtype=F32)
    decay = jnp.exp(-t * jnp.abs(deltas))
    h = h * decay[:, None, :]
    return h[:, 0], h[:, 1]


def _bidir_long_conv(u, h_f, h_b, d_bias):
    L = u.shape[1]
    k = jnp.concatenate([h_f, jnp.zeros((1, HY_W), F32), h_b[1:][::-1]], axis=0)
    kf = jnp.fft.rfft(k, n=2 * L, axis=0)
    uf32 = u.astype(F32)
    uf = jnp.fft.rfft(uf32, n=2 * L, axis=1)
    y = jnp.fft.irfft(uf * kf[None], n=2 * L, axis=1)[:, :L]
    return (y + uf32 * d_bias.astype(F32)).astype(u.dtype)


def _hyena(p_hy, lp):
    L = p_hy.shape[1]
    u = _short_conv(p_hy, lp["conv_w"], lp["conv_b"])
    x0, x1, v = jnp.split(u, 3, axis=-1)
    h_f, h_b = _implicit_filters(L, lp)
    return x0 * _bidir_long_conv(x1 * v, h_f, h_b, lp["hy_D"])


def _branch_merge(o_mla, p, lp):
    g_m = jax.nn.silu(p[..., OFF_GM:OFF_HY])
    g_h = jax.nn.silu(p[..., OFF_GH:N_IN])
    y_h = _hyena(p[..., OFF_HY:OFF_GH], lp)
    y = jnp.concatenate([rmsnorm(o_mla, lp["grp_g_mla"]) * g_m,
                         rmsnorm(y_h, lp["grp_g_hy"]) * g_h], axis=-1)
    return rmsnorm(y @ lp["w_out"], lp["post_g"])


def _layer(x, ctx, c, c_ctx, lp, cos, sin, update_ctx):
    B, L, _ = x.shape
    Lc = ctx.shape[1]
    sh_x, sc_x, g_x = [m[:, None, :] for m in _modulation(c, lp["ada_w"], lp["ada_b"])]
    sh_c, sc_c, g_c = _modulation(c_ctx, lp["ada_w"], lp["ada_b"])
    hx = rmsnorm(x, lp["pre_g"]) * (1.0 + sc_x) + sh_x
    hc = rmsnorm(ctx, lp["pre_g"]) * (1.0 + sc_c) + sh_c
    px = hx @ lp["w_in"]
    if update_ctx:
        pc = hc @ lp["w_in"]
        pc_kv = pc[..., OFF_KV:OFF_GM]
    else:
        pc_kv = hc @ lp["w_in"][:, OFF_KV:OFF_GM]
    kn_c, v_c, kr_c = _mla_kv(pc_kv, lp)
    kn_x, v_x, kr_x = _mla_kv(px[..., OFF_KV:OFF_GM], lp)
    kr_x = apply_axial_rope(kr_x, cos, sin)
    k_nope = jnp.concatenate([kn_c, kn_x], axis=1)
    k_rope = jnp.concatenate([kr_c, kr_x], axis=1)
    v = jnp.concatenate([v_c, v_x], axis=1)
    qn_x, qr_x = _mla_q(px[..., OFF_Q:OFF_KV], lp)
    qr_x = apply_axial_rope(qr_x, cos[:, None], sin[:, None])
    nb = L // Q_BLOCK

    def to_blocks(q):
        return q.reshape((B, nb, Q_BLOCK) + q.shape[2:]).swapaxes(0, 1)

    o_x = lax.map(lambda qs: _attend(qs[0], qs[1], k_nope, k_rope, v),
                  (to_blocks(qn_x), to_blocks(qr_x)))
    o_x = o_x.swapaxes(0, 1).reshape(B, L, MLA_W)
    x_new = x + g_x * _branch_merge(o_x, px, lp)
    if update_ctx:
        qn_c, qr_c = _mla_q(pc[..., OFF_Q:OFF_KV], lp)
        o_c = _attend(qn_c, qr_c, kn_c, kr_c, v_c).reshape(B, Lc, MLA_W)
        ctx = ctx + g_c * _branch_merge(o_c, pc, lp)
    return x_new, ctx


def setup_inputs(seed: int = 0) -> dict:
    key = jax.random.key(seed)
    ks = iter(jax.random.split(key, 32))

    def nrm(shape, s):
        return jax.random.normal(next(ks), shape, F32) * s

    def gain(shape):
        return 1.0 + nrm(shape, 0.05)

    return {
        "x": nrm((BATCH, SEQ, D_MODEL), 1.0),
        "c": nrm((BATCH, D_MODEL), 1.0),
        "ctx": nrm((BATCH, CTX_LEN, D_MODEL), 1.0),
        "c_ctx": nrm((D_MODEL,), 1.0),
        "ada_w": nrm((DEPTH, D_MODEL, 3 * D_MODEL), 0.5 * D_MODEL ** -0.5),
        "ada_b": nrm((DEPTH, 3 * D_MODEL), 0.01),
        "pre_g": gain((DEPTH, D_MODEL)),
        "w_in": nrm((DEPTH, D_MODEL, N_IN), D_MODEL ** -0.5),
        "q_norm_g": gain((DEPTH, Q_LORA)),
        "w_uq": nrm((DEPTH, Q_LORA, MLA_HEADS * (MLA_NOPE + MLA_ROPE)), Q_LORA ** -0.5),
        "kv_norm_g": gain((DEPTH, KV_LORA)),
        "w_ukv": nrm((DEPTH, KV_LORA, MLA_HEADS * (MLA_NOPE + MLA_V)), KV_LORA ** -0.5),
        "conv_w": nrm((DEPTH, CONV_W, 3 * HY_W), CONV_W ** -0.5),
        "conv_b": nrm((DEPTH, 3 * HY_W), 0.02),
        "filt_w1": nrm((DEPTH, FILT_EMB, FILT_HIDDEN), FILT_EMB ** -0.5),
        "filt_b1": nrm((DEPTH, FILT_HIDDEN), 0.1),
        "filt_freq": gain((DEPTH, FILT_HIDDEN)),
        "filt_w2": nrm((DEPTH, FILT_HIDDEN, FILT_HIDDEN), FILT_HIDDEN ** -0.5),
        "filt_b2": nrm((DEPTH, FILT_HIDDEN), 0.1),
        "filt_w3": nrm((DEPTH, FILT_HIDDEN, 2 * HY_W), FILT_HIDDEN ** -0.5),
        "hy_D": nrm((DEPTH, HY_W), 1.0),
        "grp_g_mla": gain((DEPTH, MLA_W)),
        "grp_g_hy": gain((DEPTH, HY_W)),
        "w_out": nrm((DEPTH, D_MIX, D_MODEL), D_MIX ** -0.5),
        "post_g": gain((DEPTH, D_MODEL)),
    }


def reference(x, c, ctx, c_ctx, ada_w, ada_b, pre_g, w_in, q_norm_g, w_uq, kv_norm_g, w_ukv,
              conv_w, conv_b, filt_w1, filt_b1, filt_freq, filt_w2, filt_b2, filt_w3, hy_D,
              grp_g_mla, grp_g_hy, w_out, post_g):
    cos, sin = _axial_rope_tables(x.shape[1])
    for l in range(DEPTH):
        lp = {
            "ada_w": ada_w[l], "ada_b": ada_b[l], "pre_g": pre_g[l], "w_in": w_in[l],
            "q_norm_g": q_norm_g[l], "w_uq": w_uq[l], "kv_norm_g": kv_norm_g[l], "w_ukv": w_ukv[l],
            "conv_w": conv_w[l], "conv_b": conv_b[l], "filt_w1": filt_w1[l], "filt_b1": filt_b1[l],
            "filt_freq": filt_freq[l], "filt_w2": filt_w2[l], "filt_b2": filt_b2[l],
            "filt_w3": filt_w3[l], "hy_D": hy_D[l], "grp_g_mla": grp_g_mla[l],
            "grp_g_hy": grp_g_hy[l], "w_out": w_out[l], "post_g": post_g[l],
        }
        x, ctx = _layer(x, ctx, c, c_ctx, lp, cos, sin, update_ctx=(l < DEPTH - 1))
    return x
```

```python
import functools
import math

import numpy as np
import jax
import jax.numpy as jnp
from jax import lax
from jax.experimental import pallas as pl
from jax.experimental.pallas import tpu as pltpu

F32 = jnp.float32
BF16 = jnp.bfloat16

D_MODEL = 2048
GRID_W = 64
N_HEADS = 8
D_NOPE = 128
D_ROPE = 64
D_V = 128
MLA_W = N_HEADS * D_V
Q_LORA = 512
KV_LORA = 256
ROPE_THETA = 10000.0
SCALE = (D_NOPE + D_ROPE) ** -0.5
HY_W = 1024
FILT_EMB = 33
FILT_HIDDEN = 64
FILT_TARGET = 1e-2
FILT_FAST_DECAY = 0.3
FILT_SLOW_DECAY = 1.5
EPS = 1e-6

P_Q = 0
P_KV = 512
P_GM = 1024
P_HY = 2048
P_GH = 5120
N_PROJ = 6144
D_QK = 256

VMEM_LIMIT = 58 * 1024 * 1024

FFT_N1 = 64
FFT_N2 = 128
FFT_H1 = FFT_N1 // 2 + 1
SUB = 8


def _cparams(*sem):
    return pltpu.CompilerParams(dimension_semantics=sem, vmem_limit_bytes=VMEM_LIMIT)


def _const_spec(shape):
    nd = len(shape)
    return pl.BlockSpec(shape, lambda *_: (0,) * nd, pipeline_mode=pl.Buffered(1))


@functools.lru_cache(maxsize=None)
def _dft_tables():
    n1 = np.arange(FFT_N1 // 2)
    k1 = np.arange(FFT_H1)
    eye = np.eye(SUB)
    ang = 2 * np.pi * np.outer(k1, n1) / FFT_N1
    kron_f = np.concatenate([np.kron(np.cos(ang), eye), np.kron(-np.sin(ang), eye)], 0)
    ck = np.where((k1 == 0) | (k1 == FFT_N1 // 2), 1.0, 2.0)
    angi = 2 * np.pi * np.outer(n1, k1) / FFT_N1
    kron_i = np.concatenate([np.kron(np.cos(angi) * ck, eye),
                             np.kron(-np.sin(angi) * ck, eye)], 1) / (FFT_N1 * FFT_N2)
    n2 = np.arange(FFT_N2)
    kk = FFT_N1 * np.arange(FFT_N2)[None, :, None] + k1[:, None, None]
    a = 2 * np.pi * kk * n2[None, None, :] / (FFT_N1 * FFT_N2)
    mr, mi = np.cos(a), -np.sin(a)
    rb = np.concatenate([np.concatenate([mr, -mi], 2), np.concatenate([mi, mr], 2)], 1)
    rbt = np.transpose(rb, (0, 2, 1))
    return (kron_f.astype(np.float32), kron_i.astype(np.float32),
            rb.astype(np.float32), rbt.astype(np.float32))


@functools.lru_cache(maxsize=None)
def _small_dft_tables(L):
    nf = L + 1
    nfp = -(-nf // SUB) * SUB
    n = np.arange(L)
    k = np.arange(nf)
    ang = 2 * np.pi * np.outer(k, n) / (2 * L)
    fwd = np.zeros((2 * nfp, L))
    fwd[:nf] = np.cos(ang)
    fwd[nfp:nfp + nf] = -np.sin(ang)
    ck = np.where((k == 0) | (k == L), 1.0, 2.0)
    inv = np.zeros((L, 2 * nfp))
    inv[:, :nf] = np.cos(ang).T * ck / (2 * L)
    inv[:, nfp:nfp + nf] = -np.sin(ang).T * ck / (2 * L)
    return fwd.astype(np.float32), inv.astype(np.float32)


@functools.lru_cache(maxsize=None)
def _filter_features(L):
    t = np.linspace(0.0, 1.0, L)[:, None]
    bands = (FILT_EMB - 1) // 2
    f = np.linspace(1e-4, bands - 1, bands)[None, :]
    wpos = (2.0 * math.pi) * np.arange(L)[:, None] / L
    z = np.concatenate([t, np.cos(f * wpos), -np.sin(f * wpos)], axis=-1)
    return z.astype(np.float32)


@functools.lru_cache(maxsize=None)
def _filter_deltas():
    d = np.linspace(math.log(FILT_TARGET) / FILT_FAST_DECAY,
                    math.log(FILT_TARGET) / FILT_SLOW_DECAY, HY_W)
    return np.abs(d)[None, :].astype(np.float32)


@functools.lru_cache(maxsize=None)
def _rope_table(n):
    rows = n // GRID_W
    row = np.repeat(np.arange(rows, dtype=np.float64), GRID_W)
    col = np.tile(np.arange(GRID_W, dtype=np.float64), rows)
    nf = D_ROPE // 4
    inv = ROPE_THETA ** (-np.arange(nf, dtype=np.float64) / nf)
    ang = np.stack([row[:, None] * inv, col[:, None] * inv], axis=1)
    cos = np.broadcast_to(np.cos(ang)[:, :, None, :], (n, 2, 2, nf)).reshape(n, D_ROPE)
    sin = np.broadcast_to(np.sin(ang)[:, :, None, :], (n, 2, 2, nf)).reshape(n, D_ROPE)
    return np.concatenate([cos, sin], -1).astype(np.float32)


def _identity_rope_table(n):
    return np.concatenate([np.ones((n, D_ROPE), np.float32), np.zeros((n, D_ROPE), np.float32)], -1)


def _rot_half_cols(w):
    w4 = w.reshape(w.shape[:-1] + (2, 2, D_ROPE // 4))
    return jnp.stack([-w4[..., 1, :], w4[..., 0, :]], axis=-2).reshape(w.shape)


def _mod_kernel(c_ref, w_ref, b_ref, o_ref):
    c = c_ref[...]
    s = c * jax.nn.sigmoid(c)
    o_ref[0] = jnp.dot(s, w_ref[0], preferred_element_type=F32) + b_ref[0]


def _modulation(cvec, ada_w, ada_b):
    depth, d, n = ada_w.shape
    r = cvec.shape[0]
    tn = 512
    return pl.pallas_call(
        _mod_kernel,
        grid=(depth, n // tn),
        in_specs=[pl.BlockSpec((r, d), lambda l, j: (0, 0)),
                  pl.BlockSpec((1, d, tn), lambda l, j: (l, 0, j)),
                  pl.BlockSpec((1, 1, tn), lambda l, j: (l, 0, j))],
        out_specs=pl.BlockSpec((1, r, tn), lambda l, j: (l, 0, j)),
        out_shape=jax.ShapeDtypeStruct((depth, r, n), F32),
        compiler_params=_cparams("arbitrary", "arbitrary"),
        name="adaln_modulation",
    )(cvec, ada_w, ada_b.reshape(depth, 1, n))


def _win_kernel(x_ref, sc_ref, sh_ref, g_ref, w_ref, o_ref, hx_ref):
    @pl.when(pl.program_id(1) == 0)
    def _():
        x = x_ref[...]
        r = lax.rsqrt(jnp.mean(x * x, axis=-1, keepdims=True) + EPS)
        hx = (x * r) * (g_ref[...] * (1.0 + sc_ref[0])) + sh_ref[0]
        hx_ref[...] = hx.astype(BF16)

    o_ref[...] = jnp.dot(hx_ref[...], w_ref[...], preferred_element_type=F32).astype(o_ref.dtype)


def _input_proj(x2d, sc, sh, pre_g, w, rows_per_mod, tm, col_lo=0, col_hi=N_PROJ, tn=1024):
    m, d = x2d.shape
    ncols = col_hi - col_lo
    tn = min(tn, ncols)
    per = rows_per_mod // tm
    j0 = col_lo // tn
    return pl.pallas_call(
        _win_kernel,
        grid=(m // tm, ncols // tn),
        in_specs=[pl.BlockSpec((tm, d), lambda i, j: (i, 0)),
                  pl.BlockSpec((1, 1, d), lambda i, j: (i // per, 0, 0)),
                  pl.BlockSpec((1, 1, d), lambda i, j: (i // per, 0, 0)),
                  pl.BlockSpec((1, d), lambda i, j: (0, 0)),
                  pl.BlockSpec((d, tn), lambda i, j: (0, j + j0))],
        out_specs=pl.BlockSpec((tm, tn), lambda i, j: (i, j)),
        out_shape=jax.ShapeDtypeStruct((m, ncols), BF16),
        scratch_shapes=[pltpu.VMEM((tm, d), BF16)],
        compiler_params=_cparams("arbitrary", "arbitrary"),
        name="prenorm_input_proj",
    )(x2d, sc, sh, pre_g, w)


def _rope_lanes(v, tab):
    t = v * tab
    return t + pltpu.roll(t, D_ROPE, 1)


def _q_kernel(p_ref, g_ref, w_ref, tab_ref, o_ref):
    p = p_ref[...].astype(F32)
    r = lax.rsqrt(jnp.mean(p * p, axis=-1, keepdims=True) + EPS)
    cq = ((p * r) * g_ref[...]).astype(BF16)
    tab = tab_ref[...]
    for h in range(N_HEADS):
        lo = h * D_QK
        res = jnp.dot(cq, w_ref[:, lo:lo + D_QK], preferred_element_type=F32)
        o_ref[:, lo:lo + D_NOPE] = (res[:, :D_NOPE] * SCALE).astype(o_ref.dtype)
        o_ref[:, lo + D_NOPE:lo + D_QK] = (_rope_lanes(res[:, D_NOPE:], tab) * SCALE).astype(o_ref.dtype)


def _q_proj(px, q_g, wq, tab, rows_per_seq, tm):
    m = px.shape[0]
    per = rows_per_seq // tm
    return pl.pallas_call(
        _q_kernel,
        grid=(m // tm,),
        in_specs=[pl.BlockSpec((tm, Q_LORA), lambda i: (i, P_Q // Q_LORA)),
                  pl.BlockSpec((1, Q_LORA), lambda i: (0, 0)),
                  pl.BlockSpec((Q_LORA, N_HEADS * D_QK), lambda i: (0, 0)),
                  pl.BlockSpec((tm, 2 * D_ROPE), lambda i: (i % per, 0))],
        out_specs=pl.BlockSpec((tm, N_HEADS * D_QK), lambda i: (i, 0)),
        out_shape=jax.ShapeDtypeStruct((m, N_HEADS * D_QK), BF16),
        compiler_params=_cparams("arbitrary"),
        name="mla_q_proj",
    )(px, q_g, wq, tab)


def _kv_body(p_ref, g_ref, w_ref, tab_ref, k_ref, v_ref):
    p = p_ref[:, :KV_LORA].astype(F32)
    r = lax.rsqrt(jnp.mean(p * p, axis=-1, keepdims=True) + EPS)
    ckv = ((p * r) * g_ref[...]).astype(BF16)
    kr = _rope_lanes(p_ref[:, KV_LORA:KV_LORA + 2 * D_ROPE].astype(F32), tab_ref[...])
    lane = lax.broadcasted_iota(jnp.int32, kr.shape, 1)
    kr = jnp.where(lane < D_ROPE, kr, 0.0).astype(k_ref.dtype)
    for h in range(N_HEADS):
        kn = jnp.dot(ckv, w_ref[:, h * D_NOPE:(h + 1) * D_NOPE], preferred_element_type=F32)
        k_ref[0, :, h * D_QK:h * D_QK + D_NOPE] = kn.astype(k_ref.dtype)
        k_ref[0, :, h * D_QK + D_NOPE:(h + 1) * D_QK] = kr
    v_ref[0] = jnp.dot(ckv, w_ref[:, N_HEADS * D_NOPE:], preferred_element_type=F32).astype(v_ref.dtype)


def _kv_kernel(pc_ref, px_ref, g_ref, w_ref, tab_ref, k_ref, v_ref):
    t = pl.program_id(1)

    @pl.when(t == 0)
    def _():
        _kv_body(pc_ref, g_ref, w_ref, tab_ref, k_ref, v_ref)

    @pl.when(t > 0)
    def _():
        _kv_body(px_ref, g_ref, w_ref, tab_ref, k_ref, v_ref)


def _kv_proj(pc_kv, px, kv_g, wkv, tab_all, bsz, lc, lx):
    tm = lc
    nt = 1 + lx // tm
    per = lx // tm
    kw = 2 * KV_LORA
    return pl.pallas_call(
        _kv_kernel,
        grid=(bsz, nt),
        in_specs=[pl.BlockSpec((tm, kw), lambda b, t: (b, 0)),
                  pl.BlockSpec((tm, kw), lambda b, t: (b * per + jnp.maximum(t - 1, 0), P_KV // kw)),
                  pl.BlockSpec((1, KV_LORA), lambda b, t: (0, 0)),
                  pl.BlockSpec((KV_LORA, N_HEADS * (D_NOPE + D_V)), lambda b, t: (0, 0)),
                  pl.BlockSpec((tm, 2 * D_ROPE), lambda b, t: (t, 0))],
        out_specs=[pl.BlockSpec((1, tm, N_HEADS * D_QK), lambda b, t: (b, t, 0)),
                   pl.BlockSpec((1, tm, MLA_W), lambda b, t: (b, t, 0))],
        out_shape=[jax.ShapeDtypeStruct((bsz, lc + lx, N_HEADS * D_QK), BF16),
                   jax.ShapeDtypeStruct((bsz, lc + lx, MLA_W), BF16)],
        compiler_params=_cparams("arbitrary", "arbitrary"),
        name="mla_kv_proj",
    )(pc_kv, px, kv_g, wkv, tab_all)


def _attn_kernel(q_ref, k_ref, v_ref, o_ref):
    q = q_ref[0]
    s = lax.dot_general(q, k_ref[0], (((1,), (1,)), ((), ())), preferred_element_type=F32)
    m = jnp.max(s, axis=-1, keepdims=True)
    p = jnp.exp(s - m)
    l = jnp.sum(p, axis=-1, keepdims=True)
    o = jnp.dot(p.astype(BF16), v_ref[0], preferred_element_type=F32)
    o_ref[0] = (o / l).astype(o_ref.dtype)


def _attention(q, k, v, lk, tq):
    bsz, lq, _ = q.shape
    return pl.pallas_call(
        _attn_kernel,
        grid=(bsz, N_HEADS, lq // tq),
        in_specs=[pl.BlockSpec((1, tq, D_QK), lambda b, h, i: (b, i, h)),
                  pl.BlockSpec((1, lk, D_QK), lambda b, h, i: (b, 0, h)),
                  pl.BlockSpec((1, lk, D_V), lambda b, h, i: (b, 0, h))],
        out_specs=pl.BlockSpec((1, tq, D_V), lambda b, h, i: (b, i, h)),
        out_shape=jax.ShapeDtypeStruct((bsz, lq, MLA_W), BF16),
        compiler_params=_cparams("arbitrary", "arbitrary", "arbitrary"),
        name="mla_attention",
    )(q, k, v)


def _filter_mlp(z_ref, w1_ref, b1_ref, fr_ref, w2_ref, b2_ref):
    hi = lax.Precision.HIGHEST
    fr = fr_ref[...]
    h = jnp.sin(fr * (jnp.dot(z_ref[...], w1_ref[...], precision=hi, preferred_element_type=F32) + b1_ref[...]))
    return jnp.sin(fr * (jnp.dot(h, w2_ref[...], precision=hi, preferred_element_type=F32) + b2_ref[...]))


def _filter_taps(h, z_ref, w3_ref, dl_ref, zero_first):
    taps = jnp.dot(h, w3_ref[...], precision=lax.Precision.HIGHEST, preferred_element_type=F32)
    taps = taps * jnp.exp(-z_ref[:, 0:1] * dl_ref[...])
    if zero_first:
        row = lax.broadcasted_iota(jnp.int32, taps.shape, 0)
        taps = jnp.where(row == 0, 0.0, taps)
    return taps


def _fill_padded(pad_ref, src, n):
    ct = pad_ref.shape[1]
    pad_ref[0:SUB, :] = jnp.zeros((SUB, ct), F32)
    pad_ref[SUB + n:2 * SUB + n, :] = jnp.zeros((SUB, ct), F32)
    pad_ref[SUB:SUB + n, :] = src.astype(F32)


def _short_conv(pad_ref, w_ref, b_ref, r0, rows):
    lo = SUB + r0
    return (b_ref[...]
            + pad_ref[lo - 1:lo - 1 + rows, :] * w_ref[0:1, :]
            + pad_ref[lo:lo + rows, :] * w_ref[1:2, :]
            + pad_ref[lo + 1:lo + 1 + rows, :] * w_ref[2:3, :])


def _cmul(ur, ui, kr, ki):
    return ur * kr - ui * ki, ur * ki + ui * kr


CONV_CHUNK = 512


def _stage1(u_ref, kf_ref, a_ref):
    n1h, _, ct = u_ref.shape
    for j in range(FFT_N2 // SUB):
        slab = u_ref[:, j * SUB:(j + 1) * SUB, :].reshape(n1h * SUB, ct).astype(BF16)
        r = jnp.dot(kf_ref[...], slab, preferred_element_type=F32)
        a_ref[:, :, j * SUB:(j + 1) * SUB, :] = r.reshape(2, FFT_H1, SUB, ct)


def _stage2(a_ref, rb_ref, i):
    ct = a_ref.shape[-1]
    x = a_ref[:, i].reshape(2 * FFT_N2, ct).astype(BF16)
    return jnp.dot(rb_ref[i], x, preferred_element_type=F32)


def _lfilt_kernel(z_ref, w1_ref, b1_ref, fr_ref, w2_ref, b2_ref, w3f_ref, w3b_ref, dl_ref,
                  kf_ref, rb_ref, o_ref, u_ref, a_ref, s_ref):
    ct = u_ref.shape[-1]
    h = _filter_mlp(z_ref, w1_ref, b1_ref, fr_ref, w2_ref, b2_ref)
    for w3_ref, back in ((w3f_ref, False), (w3b_ref, True)):
        taps = _filter_taps(h, z_ref, w3_ref, dl_ref, zero_first=back)
        u_ref[...] = taps.reshape(u_ref.shape)
        _stage1(u_ref, kf_ref, a_ref)

        def body(i, carry, back=back):
            s = _stage2(a_ref, rb_ref, i).reshape(2, FFT_N2, ct)
            if back:
                s_ref[0, i] = s_ref[0, i] + s[0]
                s_ref[1, i] = s_ref[1, i] - s[1]
            else:
                s_ref[:, i] = s
            return carry

        lax.fori_loop(0, FFT_H1, body, 0)
    o_ref[...] = s_ref[...].astype(o_ref.dtype)


def _long_filter_spectrum(lp, L, ct):
    kron_f, _, rb, _ = _dft_tables()
    z = jnp.asarray(_filter_features(L))
    dl = jnp.asarray(_filter_deltas())
    nct = HY_W // ct
    full = lambda shape: pl.BlockSpec(shape, lambda j: (0,) * len(shape))
    return pl.pallas_call(
        _lfilt_kernel,
        grid=(nct,),
        in_specs=[full((L, FILT_EMB)), full((FILT_EMB, FILT_HIDDEN)), full((1, FILT_HIDDEN)),
                  full((1, FILT_HIDDEN)), full((FILT_HIDDEN, FILT_HIDDEN)), full((1, FILT_HIDDEN)),
                  pl.BlockSpec((FILT_HIDDEN, ct), lambda j: (0, j)),
                  pl.BlockSpec((FILT_HIDDEN, ct), lambda j: (0, j + nct)),
                  pl.BlockSpec((1, ct), lambda j: (0, j)),
                  _const_spec(kron_f.shape), _const_spec(rb.shape)],
        out_specs=pl.BlockSpec((2, FFT_H1, FFT_N2, ct), lambda j: (0, 0, 0, j)),
        out_shape=jax.ShapeDtypeStruct((2, FFT_H1, FFT_N2, HY_W), BF16),
        scratch_shapes=[pltpu.VMEM((FFT_N1 // 2, FFT_N2, ct), F32),
                        pltpu.VMEM((2, FFT_H1, FFT_N2, ct), F32),
                        pltpu.VMEM((2, FFT_H1, FFT_N2, ct), F32)],
        compiler_params=_cparams("arbitrary"),
        name="hyena_filter_spectrum",
    )(z, lp["filt_w1"], lp["filt_b1"][None], lp["filt_freq"][None], lp["filt_w2"], lp["filt_b2"][None],
      lp["filt_w3"], lp["filt_w3"], dl, jnp.asarray(kron_f).astype(BF16), jnp.asarray(rb).astype(BF16))


def _lhyena_kernel(x0_ref, x1_ref, v_ref, w0_ref, w1_ref, wv_ref, b0_ref, b1_ref, bv_ref, d_ref,
                   ks_ref, kf_ref, ki_ref, rb_ref, rbt_ref, o_ref, pad_ref, u_ref, a_ref):
    n1h, n2, ct = u_ref.shape
    L = n1h * n2
    nchunk = L // CONV_CHUNK
    cpt = CONV_CHUNK // n2

    _fill_padded(pad_ref, x1_ref[0].reshape(L, ct), L)
    for c in range(nchunk):
        u_ref[c * cpt:(c + 1) * cpt] = _short_conv(pad_ref, w1_ref, b1_ref, c * CONV_CHUNK, CONV_CHUNK).reshape(cpt, n2, ct)
    _fill_padded(pad_ref, v_ref[0].reshape(L, ct), L)
    for c in range(nchunk):
        vv = _short_conv(pad_ref, wv_ref, bv_ref, c * CONV_CHUNK, CONV_CHUNK).reshape(cpt, n2, ct)
        u_ref[c * cpt:(c + 1) * cpt] = u_ref[c * cpt:(c + 1) * cpt] * vv

    _stage1(u_ref, kf_ref, a_ref)

    def body(i, carry):
        uh = _stage2(a_ref, rb_ref, i)
        kr = ks_ref[0, i].astype(F32)
        ki = ks_ref[1, i].astype(F32)
        zr, zi = _cmul(uh[:n2], uh[n2:], kr, ki)
        zz = jnp.concatenate([zr, zi], axis=0).astype(BF16)
        g = jnp.dot(rbt_ref[i], zz, preferred_element_type=F32)
        a_ref[:, i] = g.reshape(2, n2, ct)
        return carry

    lax.fori_loop(0, FFT_H1, body, 0)

    for j in range(n2 // SUB):
        g = a_ref[:, :, j * SUB:(j + 1) * SUB, :].reshape(2 * FFT_H1 * SUB, ct).astype(BF16)
        y = jnp.dot(ki_ref[...], g, preferred_element_type=F32).reshape(n1h, SUB, ct)
        o_ref[0, :, j * SUB:(j + 1) * SUB, :] = y + u_ref[:, j * SUB:(j + 1) * SUB, :] * d_ref[...]

    _fill_padded(pad_ref, x0_ref[0].reshape(L, ct), L)
    for c in range(nchunk):
        x0 = _short_conv(pad_ref, w0_ref, b0_ref, c * CONV_CHUNK, CONV_CHUNK).reshape(cpt, n2, ct)
        o_ref[0, c * cpt:(c + 1) * cpt] = o_ref[0, c * cpt:(c + 1) * cpt] * x0


def _long_hyena(px, lp, kspec, bsz, L, ct):
    kron_f, kron_i, rb, rbt = _dft_tables()
    n1h = L // FFT_N2
    px4 = px.reshape(bsz, n1h, FFT_N2, N_PROJ)
    nct = HY_W // ct
    c0 = P_HY // ct
    part = lambda k: pl.BlockSpec((1, n1h, FFT_N2, ct), lambda j, b: (b, 0, 0, c0 + k * nct + j))
    wpart = lambda k: pl.BlockSpec((3, ct), lambda j, b: (0, k * nct + j))
    bpart = lambda k: pl.BlockSpec((1, ct), lambda j, b: (0, k * nct + j))
    cb = lp["conv_b"][None]
    out = pl.pallas_call(
        _lhyena_kernel,
        grid=(nct, bsz),
        in_specs=[part(0), part(1), part(2), wpart(0), wpart(1), wpart(2), bpart(0), bpart(1), bpart(2),
                  pl.BlockSpec((1, ct), lambda j, b: (0, j)),
                  pl.BlockSpec((2, FFT_H1, FFT_N2, ct), lambda j, b: (0, 0, 0, j), pipeline_mode=pl.Buffered(1)),
                  _const_spec(kron_f.shape), _const_spec(kron_i.shape),
                  _const_spec(rb.shape), _const_spec(rbt.shape)],
        out_specs=pl.BlockSpec((1, n1h, FFT_N2, ct), lambda j, b: (b, 0, 0, j)),
        out_shape=jax.ShapeDtypeStruct((bsz, n1h, FFT_N2, HY_W), F32),
        scratch_shapes=[pltpu.VMEM((L + 2 * SUB, ct), F32),
                        pltpu.VMEM((n1h, FFT_N2, ct), F32),
                        pltpu.VMEM((2, FFT_H1, FFT_N2, ct), F32)],
        compiler_params=_cparams("arbitrary", "arbitrary"),
        name="hyena_long_conv",
    )(px4, px4, px4, lp["conv_w"], lp["conv_w"], lp["conv_w"], cb, cb, cb, lp["hy_D"][None], kspec,
      jnp.asarray(kron_f).astype(BF16), jnp.asarray(kron_i).astype(BF16),
      jnp.asarray(rb).astype(BF16), jnp.asarray(rbt).astype(BF16))
    return out.reshape(bsz * L, HY_W)


def _sfilt_kernel(z_ref, w1_ref, b1_ref, fr_ref, w2_ref, b2_ref, w3f_ref, w3b_ref, dl_ref, f_ref, o_ref):
    h = _filter_mlp(z_ref, w1_ref, b1_ref, fr_ref, w2_ref, b2_ref)
    nfp = f_ref.shape[0] // 2
    sf = jnp.dot(f_ref[...], _filter_taps(h, z_ref, w3f_ref, dl_ref, False).astype(BF16), preferred_element_type=F32)
    sb = jnp.dot(f_ref[...], _filter_taps(h, z_ref, w3b_ref, dl_ref, True).astype(BF16), preferred_element_type=F32)
    o_ref[0:nfp, :] = sf[:nfp] + sb[:nfp]
    o_ref[nfp:, :] = sf[nfp:] - sb[nfp:]


def _short_filter_spectrum(lp, L, ct):
    fwd, _ = _small_dft_tables(L)
    z = jnp.asarray(_filter_features(L))
    dl = jnp.asarray(_filter_deltas())
    nct = HY_W // ct
    full = lambda shape: pl.BlockSpec(shape, lambda j: (0,) * len(shape))
    return pl.pallas_call(
        _sfilt_kernel,
        grid=(nct,),
        in_specs=[full((L, FILT_EMB)), full((FILT_EMB, FILT_HIDDEN)), full((1, FILT_HIDDEN)),
                  full((1, FILT_HIDDEN)), full((FILT_HIDDEN, FILT_HIDDEN)), full((1, FILT_HIDDEN)),
                  pl.BlockSpec((FILT_HIDDEN, ct), lambda j: (0, j)),
                  pl.BlockSpec((FILT_HIDDEN, ct), lambda j: (0, j + nct)),
                  pl.BlockSpec((1, ct), lambda j: (0, j)),
                  full(fwd.shape)],
        out_specs=pl.BlockSpec((fwd.shape[0], ct), lambda j: (0, j)),
        out_shape=jax.ShapeDtypeStruct((fwd.shape[0], HY_W), F32),
        compiler_params=_cparams("arbitrary"),
        name="hyena_ctx_filter_spectrum",
    )(z, lp["filt_w1"], lp["filt_b1"][None], lp["filt_freq"][None], lp["filt_w2"], lp["filt_b2"][None],
      lp["filt_w3"], lp["filt_w3"], dl, jnp.asarray(fwd).astype(BF16))


def _shyena_kernel(x0_ref, x1_ref, v_ref, w0_ref, w1_ref, wv_ref, b0_ref, b1_ref, bv_ref, d_ref,
                   ks_ref, f_ref, fi_ref, o_ref, pad_ref):
    L = x0_ref.shape[0]
    nfp = f_ref.shape[0] // 2
    _fill_padded(pad_ref, x1_ref[...], L)
    u = _short_conv(pad_ref, w1_ref, b1_ref, 0, L)
    _fill_padded(pad_ref, v_ref[...], L)
    u = u * _short_conv(pad_ref, wv_ref, bv_ref, 0, L)
    uh = jnp.dot(f_ref[...], u.astype(BF16), preferred_element_type=F32)
    zr, zi = _cmul(uh[:nfp], uh[nfp:], ks_ref[0:nfp, :], ks_ref[nfp:, :])
    zz = jnp.concatenate([zr, zi], axis=0).astype(BF16)
    y = jnp.dot(fi_ref[...], zz, preferred_element_type=F32)
    _fill_padded(pad_ref, x0_ref[...], L)
    o_ref[...] = _short_conv(pad_ref, w0_ref, b0_ref, 0, L) * (y + u * d_ref[...])


def _short_hyena(pc, lp, kspec, bsz, L, ct):
    fwd, inv = _small_dft_tables(L)
    nct = HY_W // ct
    c0 = P_HY // ct
    part = lambda k: pl.BlockSpec((L, ct), lambda j, b: (b, c0 + k * nct + j))
    wpart = lambda k: pl.BlockSpec((3, ct), lambda j, b: (0, k * nct + j))
    bpart = lambda k: pl.BlockSpec((1, ct), lambda j, b: (0, k * nct + j))
    full = lambda shape: pl.BlockSpec(shape, lambda j, b: (0,) * len(shape))
    cb = lp["conv_b"][None]
    return pl.pallas_call(
        _shyena_kernel,
        grid=(nct, bsz),
        in_specs=[part(0), part(1), part(2), wpart(0), wpart(1), wpart(2), bpart(0), bpart(1), bpart(2),
                  pl.BlockSpec((1, ct), lambda j, b: (0, j)),
                  pl.BlockSpec((fwd.shape[0], ct), lambda j, b: (0, j)),
                  full(fwd.shape), full(inv.shape)],
        out_specs=pl.BlockSpec((L, ct), lambda j, b: (b, j)),
        out_shape=jax.ShapeDtypeStruct((bsz * L, HY_W), F32),
        scratch_shapes=[pltpu.VMEM((L + 2 * SUB, ct), F32)],
        compiler_params=_cparams("arbitrary", "arbitrary"),
        name="hyena_ctx_conv",
    )(pc, pc, pc, lp["conv_w"], lp["conv_w"], lp["conv_w"], cb, cb, cb, lp["hy_D"][None], kspec,
      jnp.asarray(fwd).astype(BF16), jnp.asarray(inv).astype(BF16))


def _merge_kernel(o_ref, gm_ref, y_ref, gh_ref, x_ref, gx_ref, gmla_ref, ghy_ref, pg_ref, w_ref, out_ref):
    def normed(t, g_ref, gate_ref):
        r = lax.rsqrt(jnp.mean(t * t, axis=-1, keepdims=True) + EPS)
        gate = gate_ref[...].astype(F32)
        return ((t * r) * g_ref[...] * (gate * jax.nn.sigmoid(gate))).astype(BF16)

    a = normed(o_ref[...].astype(F32), gmla_ref, gm_ref)
    b = normed(y_ref[...], ghy_ref, gh_ref)
    z = (jnp.dot(a, w_ref[:MLA_W, :], preferred_element_type=F32)
         + jnp.dot(b, w_ref[MLA_W:, :], preferred_element_type=F32))
    r = lax.rsqrt(jnp.mean(z * z, axis=-1, keepdims=True) + EPS)
    out_ref[...] = x_ref[...] + gx_ref[0] * ((z * r) * pg_ref[...])


def _merge(o, px, yh, x2d, gx, lp, w_out, rows_per_mod, tm):
    m, d = x2d.shape
    per = rows_per_mod // tm
    return pl.pallas_call(
        _merge_kernel,
        grid=(m // tm,),
        in_specs=[pl.BlockSpec((tm, MLA_W), lambda i: (i, 0)),
                  pl.BlockSpec((tm, MLA_W), lambda i: (i, P_GM // MLA_W)),
                  pl.BlockSpec((tm, HY_W), lambda i: (i, 0)),
                  pl.BlockSpec((tm, HY_W), lambda i: (i, P_GH // HY_W)),
                  pl.BlockSpec((tm, d), lambda i: (i, 0)),
                  pl.BlockSpec((1, 1, d), lambda i: (i // per, 0, 0)),
                  pl.BlockSpec((1, MLA_W), lambda i: (0, 0)),
                  pl.BlockSpec((1, HY_W), lambda i: (0, 0)),
                  pl.BlockSpec((1, d), lambda i: (0, 0)),
                  _const_spec((MLA_W + HY_W, d))],
        out_specs=pl.BlockSpec((tm, d), lambda i: (i, 0)),
        out_shape=jax.ShapeDtypeStruct((m, d), F32),
        compiler_params=_cparams("arbitrary"),
        name="branch_merge_out_proj",
    )(o, px, yh, px, x2d, gx, lp["grp_g_mla"][None], lp["grp_g_hy"][None], lp["post_g"][None], w_out)


def _prep_weights(lp):
    w_in = lp["w_in"]
    off_kv = Q_LORA
    off_kr = off_kv + KV_LORA
    off_gm = off_kr + D_ROPE
    off_hy = off_gm + MLA_W
    off_gh = off_hy + 3 * HY_W
    w_kr = w_in[:, off_kr:off_gm]
    w_in_p = jnp.concatenate([
        w_in[:, :off_kr], w_kr, _rot_half_cols(w_kr), jnp.zeros((D_MODEL, P_GM - P_KV - KV_LORA - 2 * D_ROPE), F32),
        w_in[:, off_gm:off_hy], w_in[:, off_hy:off_gh], w_in[:, off_gh:]], axis=1).astype(BF16)
    wq = lp["w_uq"].reshape(Q_LORA, N_HEADS, D_NOPE + D_ROPE)
    wq_r = wq[..., D_NOPE:]
    wq_cat = jnp.concatenate([wq[..., :D_NOPE], wq_r, _rot_half_cols(wq_r)], axis=-1)
    wq_cat = wq_cat.reshape(Q_LORA, N_HEADS * D_QK).astype(BF16)
    wkv = lp["w_ukv"].reshape(KV_LORA, N_HEADS, D_NOPE + D_V)
    wkv_cat = jnp.concatenate([wkv[..., :D_NOPE].reshape(KV_LORA, -1), wkv[..., D_NOPE:].reshape(KV_LORA, -1)],
                              axis=1).astype(BF16)
    return w_in_p, wq_cat, wkv_cat, lp["w_out"].astype(BF16)


def _layer(x2d, c2d, mod, lp, bsz, L, Lc, update_ctx):
    d = D_MODEL
    w_in_p, wq_cat, wkv_cat, w_out = _prep_weights(lp)
    sh, sc, gt = mod[:, :d], mod[:, d:2 * d], mod[:, 2 * d:]
    sh_x, sc_x, g_x = (t[:bsz, None, :] for t in (sh, sc, gt))
    sh_c, sc_c, g_c = (t[bsz:bsz + 1, None, :] for t in (sh, sc, gt))
    pre_g = lp["pre_g"][None]

    px = _input_proj(x2d, sc_x, sh_x, pre_g, w_in_p, rows_per_mod=L, tm=512)
    if update_ctx:
        pc = _input_proj(c2d, sc_c, sh_c, pre_g, w_in_p, rows_per_mod=bsz * Lc, tm=Lc)
        pc_kv = pc[:, P_KV:P_GM]
    else:
        pc_kv = _input_proj(c2d, sc_c, sh_c, pre_g, w_in_p, rows_per_mod=bsz * Lc, tm=Lc,
                            col_lo=P_KV, col_hi=P_GM)

    tab_x = _rope_table(L)
    tab_all = jnp.asarray(np.concatenate([_identity_rope_table(Lc), tab_x], 0))
    k_all, v_all = _kv_proj(pc_kv, px, lp["kv_norm_g"][None], wkv_cat, tab_all, bsz, Lc, L)
    q_x = _q_proj(px, lp["q_norm_g"][None], wq_cat, jnp.asarray(tab_x), rows_per_seq=L, tm=512)
    o_x = _attention(q_x.reshape(bsz, L, -1), k_all, v_all, lk=Lc + L, tq=256).reshape(bsz * L, MLA_W)

    kspec = _long_filter_spectrum(lp, L, ct=256)
    y_x = _long_hyena(px, lp, kspec, bsz, L, ct=256)
    x_new = _merge(o_x, px, y_x, x2d, g_x, lp, w_out, rows_per_mod=L, tm=256)

    c_new = c2d
    if update_ctx:
        q_c = _q_proj(pc, lp["q_norm_g"][None], wq_cat, jnp.asarray(_identity_rope_table(Lc)),
                      rows_per_seq=Lc, tm=Lc)
        o_c = _attention(q_c.reshape(bsz, Lc, -1), k_all, v_all, lk=Lc, tq=Lc).reshape(bsz * Lc, MLA_W)
        kspec_c = _short_filter_spectrum(lp, Lc, ct=256)
        y_c = _short_hyena(pc, lp, kspec_c, bsz, Lc, ct=256)
        c_new = _merge(o_c, pc, y_c, c2d, g_c, lp, w_out, rows_per_mod=bsz * Lc, tm=Lc)
    return x_new, c_new


def kernel(x, c, ctx, c_ctx, ada_w, ada_b, pre_g, w_in, q_norm_g, w_uq, kv_norm_g, w_ukv, conv_w, conv_b,
           filt_w1, filt_b1, filt_freq, filt_w2, filt_b2, filt_w3, hy_D, grp_g_mla, grp_g_hy, w_out, post_g):
    bsz, L, d = x.shape
    Lc = ctx.shape[1]
    depth = ada_w.shape[0]
    assert d == D_MODEL and L == (FFT_N1 // 2) * FFT_N2 and bsz + 1 <= SUB
    cvec = jnp.concatenate([c, c_ctx[None], jnp.zeros((SUB - bsz - 1, d), F32)], axis=0)
    mod = _modulation(cvec, ada_w, ada_b)
    params = dict(pre_g=pre_g, w_in=w_in, q_norm_g=q_norm_g, w_uq=w_uq, kv_norm_g=kv_norm_g, w_ukv=w_ukv,
                  conv_w=conv_w, conv_b=conv_b, filt_w1=filt_w1, filt_b1=filt_b1, filt_freq=filt_freq,
                  filt_w2=filt_w2, filt_b2=filt_b2, filt_w3=filt_w3, hy_D=hy_D, grp_g_mla=grp_g_mla,
                  grp_g_hy=grp_g_hy, w_out=w_out, post_g=post_g)
    x2d = x.reshape(bsz * L, d)
    c2d = ctx.reshape(bsz * Lc, d)
    for l in range(depth):
        lp = {k: v[l] for k, v in params.items()}
        x2d, c2d = _layer(x2d, c2d, mod[l], lp, bsz, L, Lc, update_ctx=(l < depth - 1))
    return x2d.reshape(bsz, L, d)
```

```python
import functools
import math

import numpy as np
import jax
import jax.numpy as jnp
from jax import lax
from jax.experimental import pallas as pl
from jax.experimental.pallas import tpu as pltpu

F32 = jnp.float32
BF16 = jnp.bfloat16

D_MODEL = 2048
GRID_W = 64
N_HEADS = 8
D_NOPE = 128
D_ROPE = 64
D_V = 128
MLA_W = N_HEADS * D_V
Q_LORA = 512
KV_LORA = 256
ROPE_THETA = 10000.0
SCALE = (D_NOPE + D_ROPE) ** -0.5
Q_SCALE = SCALE * math.log2(math.e)
HY_W = 1024
FILT_EMB = 33
FILT_HIDDEN = 64
FILT_TARGET = 1e-2
FILT_FAST_DECAY = 0.3
FILT_SLOW_DECAY = 1.5
EPS = 1e-6

P_Q = 0
P_KV = 512
P_GM = 1024
P_HY = 2048
P_GH = 5120
N_PROJ = 6144
D_QK = 256

VMEM_LIMIT = 58 * 1024 * 1024

FFT_N1 = 64
FFT_N2 = 128
FFT_H1 = FFT_N1 // 2 + 1
SUB = 8


def _cparams(*sem):
    return pltpu.CompilerParams(dimension_semantics=sem, vmem_limit_bytes=VMEM_LIMIT)


def _const_spec(shape):
    nd = len(shape)
    return pl.BlockSpec(shape, lambda *_: (0,) * nd, pipeline_mode=pl.Buffered(1))


@functools.lru_cache(maxsize=None)
def _dft_tables():
    n1 = np.arange(FFT_N1 // 2)
    k1 = np.arange(FFT_H1)
    eye = np.eye(SUB)
    ang = 2 * np.pi * np.outer(k1, n1) / FFT_N1
    kron_f = np.concatenate([np.kron(np.cos(ang), eye), np.kron(-np.sin(ang), eye)], 0)
    ck = np.where((k1 == 0) | (k1 == FFT_N1 // 2), 1.0, 2.0)
    angi = 2 * np.pi * np.outer(n1, k1) / FFT_N1
    kron_i = np.concatenate([np.kron(np.cos(angi) * ck, eye),
                             np.kron(-np.sin(angi) * ck, eye)], 1) / (FFT_N1 * FFT_N2)
    n2 = np.arange(FFT_N2)
    kk = FFT_N1 * np.arange(FFT_N2)[None, :, None] + k1[:, None, None]
    a = 2 * np.pi * kk * n2[None, None, :] / (FFT_N1 * FFT_N2)
    mr, mi = np.cos(a), -np.sin(a)
    rb = np.concatenate([np.concatenate([mr, -mi], 2), np.concatenate([mi, mr], 2)], 1)
    rbt = np.transpose(rb, (0, 2, 1))
    return (kron_f.astype(np.float32), kron_i.astype(np.float32),
            rb.astype(np.float32), rbt.astype(np.float32))


@functools.lru_cache(maxsize=None)
def _small_dft_tables(L):
    nf = L + 1
    nfp = -(-nf // SUB) * SUB
    n = np.arange(L)
    k = np.arange(nf)
    ang = 2 * np.pi * np.outer(k, n) / (2 * L)
    fwd = np.zeros((2 * nfp, L))
    fwd[:nf] = np.cos(ang)
    fwd[nfp:nfp + nf] = -np.sin(ang)
    ck = np.where((k == 0) | (k == L), 1.0, 2.0)
    inv = np.zeros((L, 2 * nfp))
    inv[:, :nf] = np.cos(ang).T * ck / (2 * L)
    inv[:, nfp:nfp + nf] = -np.sin(ang).T * ck / (2 * L)
    return fwd.astype(np.float32), inv.astype(np.float32)


@functools.lru_cache(maxsize=None)
def _filter_features(L):
    t = np.linspace(0.0, 1.0, L)[:, None]
    bands = (FILT_EMB - 1) // 2
    f = np.linspace(1e-4, bands - 1, bands)[None, :]
    wpos = (2.0 * math.pi) * np.arange(L)[:, None] / L
    z = np.concatenate([t, np.cos(f * wpos), -np.sin(f * wpos)], axis=-1)
    return z.astype(np.float32)


@functools.lru_cache(maxsize=None)
def _filter_deltas():
    d = np.linspace(math.log(FILT_TARGET) / FILT_FAST_DECAY,
                    math.log(FILT_TARGET) / FILT_SLOW_DECAY, HY_W)
    return np.abs(d)[None, :].astype(np.float32)


@functools.lru_cache(maxsize=None)
def _rope_table(n):
    rows = n // GRID_W
    row = np.repeat(np.arange(rows, dtype=np.float64), GRID_W)
    col = np.tile(np.arange(GRID_W, dtype=np.float64), rows)
    nf = D_ROPE // 4
    inv = ROPE_THETA ** (-np.arange(nf, dtype=np.float64) / nf)
    ang = np.stack([row[:, None] * inv, col[:, None] * inv], axis=1)
    cos = np.broadcast_to(np.cos(ang)[:, :, None, :], (n, 2, 2, nf)).reshape(n, D_ROPE)
    sin = np.broadcast_to(np.sin(ang)[:, :, None, :], (n, 2, 2, nf)).reshape(n, D_ROPE)
    return np.concatenate([cos, sin], -1).astype(np.float32)


def _identity_rope_table(n):
    return np.concatenate([np.ones((n, D_ROPE), np.float32), np.zeros((n, D_ROPE), np.float32)], -1)


def _rot_half_cols(w):
    w4 = w.reshape(w.shape[:-1] + (2, 2, D_ROPE // 4))
    return jnp.stack([-w4[..., 1, :], w4[..., 0, :]], axis=-2).reshape(w.shape)


def _mod_kernel(c_ref, w_ref, b_ref, o_ref):
    c = c_ref[...]
    s = c * jax.nn.sigmoid(c)
    o_ref[0] = jnp.dot(s, w_ref[0], preferred_element_type=F32) + b_ref[0]


def _modulation(cvec, ada_w, ada_b):
    depth, d, n = ada_w.shape
    r = cvec.shape[0]
    tn = 512
    return pl.pallas_call(
        _mod_kernel,
        grid=(depth, n // tn),
        in_specs=[pl.BlockSpec((r, d), lambda l, j: (0, 0)),
                  pl.BlockSpec((1, d, tn), lambda l, j: (l, 0, j)),
                  pl.BlockSpec((1, 1, tn), lambda l, j: (l, 0, j))],
        out_specs=pl.BlockSpec((1, r, tn), lambda l, j: (l, 0, j)),
        out_shape=jax.ShapeDtypeStruct((depth, r, n), F32),
        compiler_params=_cparams("arbitrary", "arbitrary"),
        name="adaln_modulation",
    )(cvec, ada_w, ada_b.reshape(depth, 1, n))


def _win_kernel(x_ref, sc_ref, sh_ref, g_ref, w_ref, o_ref, hx_ref):
    @pl.when(pl.program_id(1) == 0)
    def _():
        x = x_ref[...]
        r = lax.rsqrt(jnp.mean(x * x, axis=-1, keepdims=True) + EPS)
        hx = (x * r) * (g_ref[...] * (1.0 + sc_ref[0])) + sh_ref[0]
        hx_ref[...] = hx.astype(BF16)

    tn = o_ref.shape[1]
    col = pl.multiple_of(pl.program_id(1) * tn, tn)
    o_ref[...] = jnp.dot(hx_ref[...], w_ref[:, pl.ds(col, tn)], preferred_element_type=F32).astype(o_ref.dtype)


def _input_proj(x2d, sc, sh, pre_g, w, rows_per_mod, tm, col_lo=0, col_hi=N_PROJ, tn=1024):
    m, d = x2d.shape
    ncols = col_hi - col_lo
    tn = min(tn, ncols)
    per = rows_per_mod // tm
    assert col_lo % ncols == 0
    return pl.pallas_call(
        _win_kernel,
        grid=(m // tm, ncols // tn),
        in_specs=[pl.BlockSpec((tm, d), lambda i, j: (i, 0)),
                  pl.BlockSpec((1, 1, d), lambda i, j: (i // per, 0, 0)),
                  pl.BlockSpec((1, 1, d), lambda i, j: (i // per, 0, 0)),
                  pl.BlockSpec((1, d), lambda i, j: (0, 0)),
                  pl.BlockSpec((d, ncols), lambda i, j: (0, col_lo // ncols), pipeline_mode=pl.Buffered(1))],
        out_specs=pl.BlockSpec((tm, tn), lambda i, j: (i, j)),
        out_shape=jax.ShapeDtypeStruct((m, ncols), BF16),
        scratch_shapes=[pltpu.VMEM((tm, d), BF16)],
        compiler_params=_cparams("arbitrary", "arbitrary"),
        name="prenorm_input_proj",
    )(x2d, sc, sh, pre_g, w)


def _rope_lanes(v, tab):
    t = v * tab
    return t + pltpu.roll(t, D_ROPE, 1)


def _q_kernel(p_ref, g_ref, w_ref, tab_ref, o_ref):
    p = p_ref[...].astype(F32)
    r = lax.rsqrt(jnp.mean(p * p, axis=-1, keepdims=True) + EPS)
    cq = ((p * r) * g_ref[...]).astype(BF16)
    tab = tab_ref[...]
    for h in range(N_HEADS):
        lo = h * D_QK
        res = jnp.dot(cq, w_ref[:, lo:lo + D_QK], preferred_element_type=F32)
        o_ref[:, lo:lo + D_NOPE] = (res[:, :D_NOPE] * Q_SCALE).astype(o_ref.dtype)
        o_ref[:, lo + D_NOPE:lo + D_QK] = (_rope_lanes(res[:, D_NOPE:], tab) * Q_SCALE).astype(o_ref.dtype)


def _q_proj(px, q_g, wq, tab, rows_per_seq, tm):
    m = px.shape[0]
    per = rows_per_seq // tm
    return pl.pallas_call(
        _q_kernel,
        grid=(m // tm,),
        in_specs=[pl.BlockSpec((tm, Q_LORA), lambda i: (i, P_Q // Q_LORA)),
                  pl.BlockSpec((1, Q_LORA), lambda i: (0, 0)),
                  pl.BlockSpec((Q_LORA, N_HEADS * D_QK), lambda i: (0, 0)),
                  pl.BlockSpec((tm, 2 * D_ROPE), lambda i: (i % per, 0))],
        out_specs=pl.BlockSpec((tm, N_HEADS * D_QK), lambda i: (i, 0)),
        out_shape=jax.ShapeDtypeStruct((m, N_HEADS * D_QK), BF16),
        compiler_params=_cparams("arbitrary"),
        name="mla_q_proj",
    )(px, q_g, wq, tab)


def _kv_body(p_ref, g_ref, w_ref, tab_ref, k_ref, v_ref):
    p = p_ref[:, :KV_LORA].astype(F32)
    r = lax.rsqrt(jnp.mean(p * p, axis=-1, keepdims=True) + EPS)
    ckv = ((p * r) * g_ref[...]).astype(BF16)
    kr = _rope_lanes(p_ref[:, KV_LORA:KV_LORA + 2 * D_ROPE].astype(F32), tab_ref[...])
    lane = lax.broadcasted_iota(jnp.int32, kr.shape, 1)
    kr = jnp.where(lane < D_ROPE, kr, 0.0).astype(k_ref.dtype)
    for h in range(N_HEADS):
        kn = jnp.dot(ckv, w_ref[:, h * D_NOPE:(h + 1) * D_NOPE], preferred_element_type=F32)
        k_ref[0, :, h * D_QK:h * D_QK + D_NOPE] = kn.astype(k_ref.dtype)
        k_ref[0, :, h * D_QK + D_NOPE:(h + 1) * D_QK] = kr
    v_ref[0] = jnp.dot(ckv, w_ref[:, N_HEADS * D_NOPE:], preferred_element_type=F32).astype(v_ref.dtype)


def _kv_kernel(pc_ref, px_ref, g_ref, w_ref, tab_ref, k_ref, v_ref):
    t = pl.program_id(1)

    @pl.when(t == 0)
    def _():
        _kv_body(pc_ref, g_ref, w_ref, tab_ref, k_ref, v_ref)

    @pl.when(t > 0)
    def _():
        _kv_body(px_ref, g_ref, w_ref, tab_ref, k_ref, v_ref)


def _kv_proj(pc_kv, px, kv_g, wkv, tab_all, bsz, lc, lx):
    tm = lc
    nt = 1 + lx // tm
    per = lx // tm
    kw = 2 * KV_LORA
    return pl.pallas_call(
        _kv_kernel,
        grid=(bsz, nt),
        in_specs=[pl.BlockSpec((tm, kw), lambda b, t: (b, 0)),
                  pl.BlockSpec((tm, kw), lambda b, t: (b * per + jnp.maximum(t - 1, 0), P_KV // kw)),
                  pl.BlockSpec((1, KV_LORA), lambda b, t: (0, 0)),
                  pl.BlockSpec((KV_LORA, N_HEADS * (D_NOPE + D_V)), lambda b, t: (0, 0)),
                  pl.BlockSpec((tm, 2 * D_ROPE), lambda b, t: (t, 0))],
        out_specs=[pl.BlockSpec((1, tm, N_HEADS * D_QK), lambda b, t: (b, t, 0)),
                   pl.BlockSpec((1, tm, MLA_W), lambda b, t: (b, t, 0))],
        out_shape=[jax.ShapeDtypeStruct((bsz, lc + lx, N_HEADS * D_QK), BF16),
                   jax.ShapeDtypeStruct((bsz, lc + lx, MLA_W), BF16)],
        compiler_params=_cparams("arbitrary", "arbitrary"),
        name="mla_kv_proj",
    )(pc_kv, px, kv_g, wkv, tab_all)


def _attn_kernel(q_ref, k_ref, v_ref, o_ref, *, n_sub):
    rows = q_ref.shape[1] // n_sub
    for t in range(n_sub):
        q = q_ref[0, t * rows:(t + 1) * rows, :]
        s = lax.dot_general(q, k_ref[0], (((1,), (1,)), ((), ())), preferred_element_type=F32)
        m = jnp.max(s, axis=-1, keepdims=True)
        p = jnp.exp2(s - m)
        l = jnp.sum(p, axis=-1, keepdims=True)
        o = jnp.dot(p.astype(BF16), v_ref[0], preferred_element_type=F32)
        o_ref[0, t * rows:(t + 1) * rows, :] = (o / l).astype(o_ref.dtype)


def _attention(q, k, v, lk, tq, n_sub=1):
    bsz, lq, _ = q.shape
    return pl.pallas_call(
        functools.partial(_attn_kernel, n_sub=n_sub),
        grid=(bsz, N_HEADS, lq // tq),
        in_specs=[pl.BlockSpec((1, tq, D_QK), lambda b, h, i: (b, i, h)),
                  pl.BlockSpec((1, lk, D_QK), lambda b, h, i: (b, 0, h)),
                  pl.BlockSpec((1, lk, D_V), lambda b, h, i: (b, 0, h))],
        out_specs=pl.BlockSpec((1, tq, D_V), lambda b, h, i: (b, i, h)),
        out_shape=jax.ShapeDtypeStruct((bsz, lq, MLA_W), BF16),
        compiler_params=_cparams("arbitrary", "arbitrary", "arbitrary"),
        name="mla_attention",
    )(q, k, v)


def _filter_mlp(z_ref, w1_ref, b1_ref, fr_ref, w2_ref, b2_ref):
    hi = lax.Precision.HIGHEST
    fr = fr_ref[...]
    h = jnp.sin(fr * (jnp.dot(z_ref[...], w1_ref[...], precision=hi, preferred_element_type=F32) + b1_ref[...]))
    return jnp.sin(fr * (jnp.dot(h, w2_ref[...], precision=hi, preferred_element_type=F32) + b2_ref[...]))


def _filter_taps(h, z_ref, w3_ref, dl_ref, zero_first):
    taps = jnp.dot(h.astype(BF16), w3_ref[...].astype(BF16), preferred_element_type=F32)
    taps = taps * jnp.exp(-z_ref[:, 0:1] * dl_ref[...])
    if zero_first:
        row = lax.broadcasted_iota(jnp.int32, taps.shape, 0)
        taps = jnp.where(row == 0, 0.0, taps)
    return taps


def _fill_padded(pad_ref, src, n):
    ct = pad_ref.shape[1]
    pad_ref[0:SUB, :] = jnp.zeros((SUB, ct), F32)
    pad_ref[SUB + n:2 * SUB + n, :] = jnp.zeros((SUB, ct), F32)
    pad_ref[SUB:SUB + n, :] = src.astype(F32)


def _short_conv(pad_ref, w_ref, b_ref, r0, rows):
    n = rows + 2 * SUB
    win = pad_ref[r0:r0 + n, :]
    prev = pltpu.roll(win, 1, 0)[SUB:SUB + rows]
    nxt = pltpu.roll(win, n - 1, 0)[SUB:SUB + rows]
    return (b_ref[...] + prev * w_ref[0:1, :] + win[SUB:SUB + rows] * w_ref[1:2, :] + nxt * w_ref[2:3, :])


def _cmul(ur, ui, kr, ki):
    return ur * kr - ui * ki, ur * ki + ui * kr


CONV_CHUNK = 512
MID_UNROLL = 3


def _stage1(u_ref, kf_ref, a_ref):
    n1h, _, ct = u_ref.shape
    for j in range(FFT_N2 // SUB):
        slab = u_ref[:, j * SUB:(j + 1) * SUB, :].reshape(n1h * SUB, ct).astype(BF16)
        r = jnp.dot(kf_ref[...], slab, preferred_element_type=F32)
        a_ref[:, :, j * SUB:(j + 1) * SUB, :] = r.reshape(2, FFT_H1, SUB, ct)


def _stage2(a_ref, rb_ref, i):
    ct = a_ref.shape[-1]
    x = a_ref[:, i].reshape(2 * FFT_N2, ct).astype(BF16)
    return jnp.dot(rb_ref[i], x, preferred_element_type=F32)


def _lfilt_kernel(z_ref, w1_ref, b1_ref, fr_ref, w2_ref, b2_ref, w3f_ref, w3b_ref, dl_ref,
                  kf_ref, rb_ref, o_ref, h_ref, u_ref, a_ref, s_ref):
    ct = u_ref.shape[-1]

    @pl.when(pl.program_id(0) == 0)
    def _():
        h_ref[...] = _filter_mlp(z_ref, w1_ref, b1_ref, fr_ref, w2_ref, b2_ref)

    for w3_ref, back in ((w3f_ref, False), (w3b_ref, True)):
        taps = _filter_taps(h_ref[...], z_ref, w3_ref, dl_ref, zero_first=back)
        u_ref[...] = taps.reshape(u_ref.shape)
        _stage1(u_ref, kf_ref, a_ref)

        def body(i, carry, back=back):
            s = _stage2(a_ref, rb_ref, i).reshape(2, FFT_N2, ct)
            if back:
                o_ref[0, i] = (s_ref[0, i] + s[0]).astype(o_ref.dtype)
                o_ref[1, i] = (s_ref[1, i] - s[1]).astype(o_ref.dtype)
            else:
                s_ref[:, i] = s
            return carry

        lax.fori_loop(0, FFT_H1, body, 0, unroll=MID_UNROLL)


def _long_filter_spectrum(lp, L, ct):
    kron_f, _, rb, _ = _dft_tables()
    z = jnp.asarray(_filter_features(L))
    dl = jnp.asarray(_filter_deltas())
    nct = HY_W // ct
    full = lambda shape: pl.BlockSpec(shape, lambda j: (0,) * len(shape))
    return pl.pallas_call(
        _lfilt_kernel,
        grid=(nct,),
        in_specs=[full((L, FILT_EMB)), full((FILT_EMB, FILT_HIDDEN)), full((1, FILT_HIDDEN)),
                  full((1, FILT_HIDDEN)), full((FILT_HIDDEN, FILT_HIDDEN)), full((1, FILT_HIDDEN)),
                  pl.BlockSpec((FILT_HIDDEN, ct), lambda j: (0, j)),
                  pl.BlockSpec((FILT_HIDDEN, ct), lambda j: (0, j + nct)),
                  pl.BlockSpec((1, ct), lambda j: (0, j)),
                  _const_spec(kron_f.shape), _const_spec(rb.shape)],
        out_specs=pl.BlockSpec((2, FFT_H1, FFT_N2, ct), lambda j: (0, 0, 0, j)),
        out_shape=jax.ShapeDtypeStruct((2, FFT_H1, FFT_N2, HY_W), BF16),
        scratch_shapes=[pltpu.VMEM((L, FILT_HIDDEN), F32),
                        pltpu.VMEM((FFT_N1 // 2, FFT_N2, ct), F32),
                        pltpu.VMEM((2, FFT_H1, FFT_N2, ct), F32),
                        pltpu.VMEM((2, FFT_H1, FFT_N2, ct), F32)],
        compiler_params=_cparams("arbitrary"),
        name="hyena_filter_spectrum",
    )(z, lp["filt_w1"], lp["filt_b1"][None], lp["filt_freq"][None], lp["filt_w2"], lp["filt_b2"][None],
      lp["filt_w3"], lp["filt_w3"], dl, jnp.asarray(kron_f).astype(BF16), jnp.asarray(rb).astype(BF16))


def _lhyena_kernel(x0_ref, x1_ref, v_ref, w0_ref, w1_ref, wv_ref, b0_ref, b1_ref, bv_ref, d_ref,
                   ks_ref, kf_ref, ki_ref, rb_ref, rbt_ref, o_ref, pad_ref, u_ref, a_ref):
    n1h, n2, ct = u_ref.shape
    L = n1h * n2
    nchunk = L // CONV_CHUNK
    cpt = CONV_CHUNK // n2

    _fill_padded(pad_ref, x1_ref[0].reshape(L, ct), L)
    for c in range(nchunk):
        u_ref[c * cpt:(c + 1) * cpt] = _short_conv(pad_ref, w1_ref, b1_ref, c * CONV_CHUNK, CONV_CHUNK).reshape(cpt, n2, ct)
    _fill_padded(pad_ref, v_ref[0].reshape(L, ct), L)
    for c in range(nchunk):
        vv = _short_conv(pad_ref, wv_ref, bv_ref, c * CONV_CHUNK, CONV_CHUNK).reshape(cpt, n2, ct)
        u_ref[c * cpt:(c + 1) * cpt] = u_ref[c * cpt:(c + 1) * cpt] * vv

    _stage1(u_ref, kf_ref, a_ref)

    def body(i, carry):
        uh = _stage2(a_ref, rb_ref, i)
        kr = ks_ref[0, i].astype(F32)
        ki = ks_ref[1, i].astype(F32)
        zr, zi = _cmul(uh[:n2], uh[n2:], kr, ki)
        zz = jnp.concatenate([zr, zi], axis=0).astype(BF16)
        g = jnp.dot(rbt_ref[i], zz, preferred_element_type=F32)
        a_ref[:, i] = g.reshape(2, n2, ct)
        return carry

    lax.fori_loop(0, FFT_H1, body, 0, unroll=MID_UNROLL)

    for j in range(n2 // SUB):
        g = a_ref[:, :, j * SUB:(j + 1) * SUB, :].reshape(2 * FFT_H1 * SUB, ct).astype(BF16)
        y = jnp.dot(ki_ref[...], g, preferred_element_type=F32).reshape(n1h, SUB, ct)
        o_ref[0, :, j * SUB:(j + 1) * SUB, :] = y + u_ref[:, j * SUB:(j + 1) * SUB, :] * d_ref[...]

    _fill_padded(pad_ref, x0_ref[0].reshape(L, ct), L)
    for c in range(nchunk):
        x0 = _short_conv(pad_ref, w0_ref, b0_ref, c * CONV_CHUNK, CONV_CHUNK).reshape(cpt, n2, ct)
        o_ref[0, c * cpt:(c + 1) * cpt] = o_ref[0, c * cpt:(c + 1) * cpt] * x0


def _long_hyena(px, lp, kspec, bsz, L, ct):
    kron_f, kron_i, rb, rbt = _dft_tables()
    n1h = L // FFT_N2
    px4 = px.reshape(bsz, n1h, FFT_N2, N_PROJ)
    nct = HY_W // ct
    c0 = P_HY // ct
    part = lambda k: pl.BlockSpec((1, n1h, FFT_N2, ct), lambda j, b: (b, 0, 0, c0 + k * nct + j))
    wpart = lambda k: pl.BlockSpec((3, ct), lambda j, b: (0, k * nct + j))
    bpart = lambda k: pl.BlockSpec((1, ct), lambda j, b: (0, k * nct + j))
    cb = lp["conv_b"][None]
    out = pl.pallas_call(
        _lhyena_kernel,
        grid=(nct, bsz),
        in_specs=[part(0), part(1), part(2), wpart(0), wpart(1), wpart(2), bpart(0), bpart(1), bpart(2),
                  pl.BlockSpec((1, ct), lambda j, b: (0, j)),
                  pl.BlockSpec((2, FFT_H1, FFT_N2, ct), lambda j, b: (0, 0, 0, j), pipeline_mode=pl.Buffered(1)),
                  _const_spec(kron_f.shape), _const_spec(kron_i.shape),
                  _const_spec(rb.shape), _const_spec(rbt.shape)],
        out_specs=pl.BlockSpec((1, n1h, FFT_N2, ct), lambda j, b: (b, 0, 0, j)),
        out_shape=jax.ShapeDtypeStruct((bsz, n1h, FFT_N2, HY_W), F32),
        scratch_shapes=[pltpu.VMEM((L + 2 * SUB, ct), F32),
                        pltpu.VMEM((n1h, FFT_N2, ct), F32),
                        pltpu.VMEM((2, FFT_H1, FFT_N2, ct), F32)],
        compiler_params=_cparams("arbitrary", "arbitrary"),
        name="hyena_long_conv",
    )(px4, px4, px4, lp["conv_w"], lp["conv_w"], lp["conv_w"], cb, cb, cb, lp["hy_D"][None], kspec,
      jnp.asarray(kron_f).astype(BF16), jnp.asarray(kron_i).astype(BF16),
      jnp.asarray(rb).astype(BF16), jnp.asarray(rbt).astype(BF16))
    return out.reshape(bsz * L, HY_W)


def _sfilt_kernel(z_ref, w1_ref, b1_ref, fr_ref, w2_ref, b2_ref, w3f_ref, w3b_ref, dl_ref, f_ref, o_ref):
    h = _filter_mlp(z_ref, w1_ref, b1_ref, fr_ref, w2_ref, b2_ref)
    nfp = f_ref.shape[0] // 2
    sf = jnp.dot(f_ref[...], _filter_taps(h, z_ref, w3f_ref, dl_ref, False).astype(BF16), preferred_element_type=F32)
    sb = jnp.dot(f_ref[...], _filter_taps(h, z_ref, w3b_ref, dl_ref, True).astype(BF16), preferred_element_type=F32)
    o_ref[0:nfp, :] = sf[:nfp] + sb[:nfp]
    o_ref[nfp:, :] = sf[nfp:] - sb[nfp:]


def _short_filter_spectrum(lp, L, ct):
    fwd, _ = _small_dft_tables(L)
    z = jnp.asarray(_filter_features(L))
    dl = jnp.asarray(_filter_deltas())
    nct = HY_W // ct
    full = lambda shape: pl.BlockSpec(shape, lambda j: (0,) * len(shape))
    return pl.pallas_call(
        _sfilt_kernel,
        grid=(nct,),
        in_specs=[full((L, FILT_EMB)), full((FILT_EMB, FILT_HIDDEN)), full((1, FILT_HIDDEN)),
                  full((1, FILT_HIDDEN)), full((FILT_HIDDEN, FILT_HIDDEN)), full((1, FILT_HIDDEN)),
                  pl.BlockSpec((FILT_HIDDEN, ct), lambda j: (0, j)),
                  pl.BlockSpec((FILT_HIDDEN, ct), lambda j: (0, j + nct)),
                  pl.BlockSpec((1, ct), lambda j: (0, j)),
                  full(fwd.shape)],
        out_specs=pl.BlockSpec((fwd.shape[0], ct), lambda j: (0, j)),
        out_shape=jax.ShapeDtypeStruct((fwd.shape[0], HY_W), F32),
        compiler_params=_cparams("arbitrary"),
        name="hyena_ctx_filter_spectrum",
    )(z, lp["filt_w1"], lp["filt_b1"][None], lp["filt_freq"][None], lp["filt_w2"], lp["filt_b2"][None],
      lp["filt_w3"], lp["filt_w3"], dl, jnp.asarray(fwd).astype(BF16))


def _shyena_kernel(x0_ref, x1_ref, v_ref, w0_ref, w1_ref, wv_ref, b0_ref, b1_ref, bv_ref, d_ref,
                   ks_ref, f_ref, fi_ref, o_ref, pad_ref):
    L = x0_ref.shape[0]
    nfp = f_ref.shape[0] // 2
    _fill_padded(pad_ref, x1_ref[...], L)
    u = _short_conv(pad_ref, w1_ref, b1_ref, 0, L)
    _fill_padded(pad_ref, v_ref[...], L)
    u = u * _short_conv(pad_ref, wv_ref, bv_ref, 0, L)
    uh = jnp.dot(f_ref[...], u.astype(BF16), preferred_element_type=F32)
    zr, zi = _cmul(uh[:nfp], uh[nfp:], ks_ref[0:nfp, :], ks_ref[nfp:, :])
    zz = jnp.concatenate([zr, zi], axis=0).astype(BF16)
    y = jnp.dot(fi_ref[...], zz, preferred_element_type=F32)
    _fill_padded(pad_ref, x0_ref[...], L)
    o_ref[...] = _short_conv(pad_ref, w0_ref, b0_ref, 0, L) * (y + u * d_ref[...])


def _short_hyena(pc, lp, kspec, bsz, L, ct):
    fwd, inv = _small_dft_tables(L)
    nct = HY_W // ct
    c0 = P_HY // ct
    part = lambda k: pl.BlockSpec((L, ct), lambda j, b: (b, c0 + k * nct + j))
    wpart = lambda k: pl.BlockSpec((3, ct), lambda j, b: (0, k * nct + j))
    bpart = lambda k: pl.BlockSpec((1, ct), lambda j, b: (0, k * nct + j))
    full = lambda shape: pl.BlockSpec(shape, lambda j, b: (0,) * len(shape))
    cb = lp["conv_b"][None]
    return pl.pallas_call(
        _shyena_kernel,
        grid=(nct, bsz),
        in_specs=[part(0), part(1), part(2), wpart(0), wpart(1), wpart(2), bpart(0), bpart(1), bpart(2),
                  pl.BlockSpec((1, ct), lambda j, b: (0, j)),
                  pl.BlockSpec((fwd.shape[0], ct), lambda j, b: (0, j)),
                  full(fwd.shape), full(inv.shape)],
        out_specs=pl.BlockSpec((L, ct), lambda j, b: (b, j)),
        out_shape=jax.ShapeDtypeStruct((bsz * L, HY_W), F32),
        scratch_shapes=[pltpu.VMEM((L + 2 * SUB, ct), F32)],
        compiler_params=_cparams("arbitrary", "arbitrary"),
        name="hyena_ctx_conv",
    )(pc, pc, pc, lp["conv_w"], lp["conv_w"], lp["conv_w"], cb, cb, cb, lp["hy_D"][None], kspec,
      jnp.asarray(fwd).astype(BF16), jnp.asarray(inv).astype(BF16))


def _merge_kernel(o_ref, gm_ref, y_ref, gh_ref, x_ref, gx_ref, gmla_ref, ghy_ref, pg_ref, w_ref, out_ref):
    def normed(t, g_ref, gate_ref):
        r = lax.rsqrt(jnp.mean(t * t, axis=-1, keepdims=True) + EPS)
        gate = gate_ref[...].astype(F32)
        return ((t * r) * g_ref[...] * (gate * jax.nn.sigmoid(gate))).astype(BF16)

    a = normed(o_ref[...].astype(F32), gmla_ref, gm_ref)
    b = normed(y_ref[...], ghy_ref, gh_ref)
    z = (jnp.dot(a, w_ref[:MLA_W, :], preferred_element_type=F32)
         + jnp.dot(b, w_ref[MLA_W:, :], preferred_element_type=F32))
    r = lax.rsqrt(jnp.mean(z * z, axis=-1, keepdims=True) + EPS)
    out_ref[...] = x_ref[...] + gx_ref[0] * ((z * r) * pg_ref[...])


def _merge(o, px, yh, x2d, gx, lp, w_out, rows_per_mod, tm):
    m, d = x2d.shape
    per = rows_per_mod // tm
    return pl.pallas_call(
        _merge_kernel,
        grid=(m // tm,),
        in_specs=[pl.BlockSpec((tm, MLA_W), lambda i: (i, 0)),
                  pl.BlockSpec((tm, MLA_W), lambda i: (i, P_GM // MLA_W)),
                  pl.BlockSpec((tm, HY_W), lambda i: (i, 0)),
                  pl.BlockSpec((tm, HY_W), lambda i: (i, P_GH // HY_W)),
                  pl.BlockSpec((tm, d), lambda i: (i, 0)),
                  pl.BlockSpec((1, 1, d), lambda i: (i // per, 0, 0)),
                  pl.BlockSpec((1, MLA_W), lambda i: (0, 0)),
                  pl.BlockSpec((1, HY_W), lambda i: (0, 0)),
                  pl.BlockSpec((1, d), lambda i: (0, 0)),
                  _const_spec((MLA_W + HY_W, d))],
        out_specs=pl.BlockSpec((tm, d), lambda i: (i, 0)),
        out_shape=jax.ShapeDtypeStruct((m, d), F32),
        compiler_params=_cparams("arbitrary"),
        name="branch_merge_out_proj",
    )(o, px, yh, px, x2d, gx, lp["grp_g_mla"][None], lp["grp_g_hy"][None], lp["post_g"][None], w_out)


def _prep_weights(lp):
    w_in = lp["w_in"]
    off_kv = Q_LORA
    off_kr = off_kv + KV_LORA
    off_gm = off_kr + D_ROPE
    off_hy = off_gm + MLA_W
    off_gh = off_hy + 3 * HY_W
    w_in = w_in.astype(BF16)
    w_kr = w_in[:, off_kr:off_gm]
    w_in_p = jnp.concatenate([
        w_in[:, :off_gm], _rot_half_cols(w_kr), jnp.zeros((D_MODEL, P_GM - P_KV - KV_LORA - 2 * D_ROPE), BF16),
        w_in[:, off_gm:]], axis=1)
    wq = lp["w_uq"].reshape(Q_LORA, N_HEADS, D_NOPE + D_ROPE)
    wq_r = wq[..., D_NOPE:]
    wq_cat = jnp.concatenate([wq[..., :D_NOPE], wq_r, _rot_half_cols(wq_r)], axis=-1)
    wq_cat = wq_cat.reshape(Q_LORA, N_HEADS * D_QK).astype(BF16)
    wkv = lp["w_ukv"].reshape(KV_LORA, N_HEADS, D_NOPE + D_V)
    wkv_cat = jnp.concatenate([wkv[..., :D_NOPE].reshape(KV_LORA, -1), wkv[..., D_NOPE:].reshape(KV_LORA, -1)],
                              axis=1).astype(BF16)
    return w_in_p, wq_cat, wkv_cat, lp["w_out"].astype(BF16)


def _layer(x2d, c2d, mod, lp, bsz, L, Lc, update_ctx):
    d = D_MODEL
    w_in_p, wq_cat, wkv_cat, w_out = _prep_weights(lp)
    sh, sc, gt = mod[:, :d], mod[:, d:2 * d], mod[:, 2 * d:]
    sh_x, sc_x, g_x = (t[:bsz, None, :] for t in (sh, sc, gt))
    sh_c, sc_c, g_c = (t[bsz:bsz + 1, None, :] for t in (sh, sc, gt))
    pre_g = lp["pre_g"][None]

    px = _input_proj(x2d, sc_x, sh_x, pre_g, w_in_p, rows_per_mod=L, tm=1024)
    if update_ctx:
        pc = _input_proj(c2d, sc_c, sh_c, pre_g, w_in_p, rows_per_mod=bsz * Lc, tm=Lc)
        pc_kv = pc[:, P_KV:P_GM]
    else:
        pc_kv = _input_proj(c2d, sc_c, sh_c, pre_g, w_in_p, rows_per_mod=bsz * Lc, tm=Lc,
                            col_lo=P_KV, col_hi=P_GM)

    tab_x = _rope_table(L)
    tab_all = jnp.asarray(np.concatenate([_identity_rope_table(Lc), tab_x], 0))
    k_all, v_all = _kv_proj(pc_kv, px, lp["kv_norm_g"][None], wkv_cat, tab_all, bsz, Lc, L)
    q_x = _q_proj(px, lp["q_norm_g"][None], wq_cat, jnp.asarray(tab_x), rows_per_seq=L, tm=512)
    o_x = _attention(q_x.reshape(bsz, L, -1), k_all, v_all, lk=Lc + L, tq=1024, n_sub=4).reshape(bsz * L, MLA_W)

    kspec = _long_filter_spectrum(lp, L, ct=256)
    y_x = _long_hyena(px, lp, kspec, bsz, L, ct=256)
    x_new = _merge(o_x, px, y_x, x2d, g_x, lp, w_out, rows_per_mod=L, tm=512)

    c_new = c2d
    if update_ctx:
        q_c = _q_proj(pc, lp["q_norm_g"][None], wq_cat, jnp.asarray(_identity_rope_table(Lc)),
                      rows_per_seq=Lc, tm=Lc)
        o_c = _attention(q_c.reshape(bsz, Lc, -1), k_all, v_all, lk=Lc, tq=Lc).reshape(bsz * Lc, MLA_W)
        kspec_c = _short_filter_spectrum(lp, Lc, ct=256)
        y_c = _short_hyena(pc, lp, kspec_c, bsz, Lc, ct=256)
        c_new = _merge(o_c, pc, y_c, c2d, g_c, lp, w_out, rows_per_mod=bsz * Lc, tm=Lc)
    return x_new, c_new


def kernel(x, c, ctx, c_ctx, ada_w, ada_b, pre_g, w_in, q_norm_g, w_uq, kv_norm_g, w_ukv, conv_w, conv_b,
           filt_w1, filt_b1, filt_freq, filt_w2, filt_b2, filt_w3, hy_D, grp_g_mla, grp_g_hy, w_out, post_g):
    bsz, L, d = x.shape
    Lc = ctx.shape[1]
    depth = ada_w.shape[0]
    assert d == D_MODEL and L == (FFT_N1 // 2) * FFT_N2 and bsz + 1 <= SUB
    cvec = jnp.concatenate([c, c_ctx[None], jnp.zeros((SUB - bsz - 1, d), F32)], axis=0)
    mod = _modulation(cvec, ada_w, ada_b)
    params = dict(pre_g=pre_g, w_in=w_in, q_norm_g=q_norm_g, w_uq=w_uq, kv_norm_g=kv_norm_g, w_ukv=w_ukv,
                  conv_w=conv_w, conv_b=conv_b, filt_w1=filt_w1, filt_b1=filt_b1, filt_freq=filt_freq,
                  filt_w2=filt_w2, filt_b2=filt_b2, filt_w3=filt_w3, hy_D=hy_D, grp_g_mla=grp_g_mla,
                  grp_g_hy=grp_g_hy, w_out=w_out, post_g=post_g)
    x2d = x.reshape(bsz * L, d)
    c2d = ctx.reshape(bsz * Lc, d)
    for l in range(depth):
        lp = {k: v[l] for k, v in params.items()}
        x2d, c2d = _layer(x2d, c2d, mod[l], lp, bsz, L, Lc, update_ctx=(l < depth - 1))
    return x2d.reshape(bsz, L, d)
```

```python
import functools
import math

import numpy as np
import jax
import jax.numpy as jnp
from jax import lax
from jax.experimental import pallas as pl
from jax.experimental.pallas import tpu as pltpu

F32 = jnp.float32
BF16 = jnp.bfloat16

D_MODEL = 2048
GRID_W = 64
N_HEADS = 8
D_NOPE = 128
D_ROPE = 64
D_V = 128
MLA_W = N_HEADS * D_V
Q_LORA = 512
KV_LORA = 256
ROPE_THETA = 10000.0
SCALE = (D_NOPE + D_ROPE) ** -0.5
Q_SCALE = SCALE * math.log2(math.e)
HY_W = 1024
FILT_EMB = 33
FILT_HIDDEN = 64
FILT_TARGET = 1e-2
FILT_FAST_DECAY = 0.3
FILT_SLOW_DECAY = 1.5
EPS = 1e-6

P_Q = 0
P_KV = 512
P_GM = 1024
P_HY = 2048
P_GH = 5120
N_PROJ = 6144
D_QK = 256

VMEM_LIMIT = 58 * 1024 * 1024

FFT_N1 = 64
FFT_N2 = 128
FFT_H1 = FFT_N1 // 2 + 1
SUB = 8


def _cparams(*sem):
    return pltpu.CompilerParams(dimension_semantics=sem, vmem_limit_bytes=VMEM_LIMIT)


def _const_spec(shape):
    nd = len(shape)
    return pl.BlockSpec(shape, lambda *_: (0,) * nd, pipeline_mode=pl.Buffered(1))


@functools.lru_cache(maxsize=None)
def _dft_tables():
    n1 = np.arange(FFT_N1 // 2)
    k1 = np.arange(FFT_H1)
    eye = np.eye(SUB)
    ang = 2 * np.pi * np.outer(k1, n1) / FFT_N1
    kron_f = np.concatenate([np.kron(np.cos(ang), eye), np.kron(-np.sin(ang), eye)], 0)
    ck = np.where((k1 == 0) | (k1 == FFT_N1 // 2), 1.0, 2.0)
    angi = 2 * np.pi * np.outer(n1, k1) / FFT_N1
    kron_i = np.concatenate([np.kron(np.cos(angi) * ck, eye),
                             np.kron(-np.sin(angi) * ck, eye)], 1) / (FFT_N1 * FFT_N2)
    n2 = np.arange(FFT_N2)
    kk = FFT_N1 * np.arange(FFT_N2)[None, :, None] + k1[:, None, None]
    a = 2 * np.pi * kk * n2[None, None, :] / (FFT_N1 * FFT_N2)
    mr, mi = np.cos(a), -np.sin(a)
    rb = np.concatenate([np.concatenate([mr, -mi], 2), np.concatenate([mi, mr], 2)], 1)
    rbt = np.transpose(rb, (0, 2, 1))
    return (kron_f.astype(np.float32), kron_i.astype(np.float32),
            rb.astype(np.float32), rbt.astype(np.float32))


@functools.lru_cache(maxsize=None)
def _small_dft_tables(L):
    nf = L + 1
    nfp = -(-nf // SUB) * SUB
    n = np.arange(L)
    k = np.arange(nf)
    ang = 2 * np.pi * np.outer(k, n) / (2 * L)
    fwd = np.zeros((2 * nfp, L))
    fwd[:nf] = np.cos(ang)
    fwd[nfp:nfp + nf] = -np.sin(ang)
    ck = np.where((k == 0) | (k == L), 1.0, 2.0)
    inv = np.zeros((L, 2 * nfp))
    inv[:, :nf] = np.cos(ang).T * ck / (2 * L)
    inv[:, nfp:nfp + nf] = -np.sin(ang).T * ck / (2 * L)
    return fwd.astype(np.float32), inv.astype(np.float32)


@functools.lru_cache(maxsize=None)
def _filter_features(L):
    t = np.linspace(0.0, 1.0, L)[:, None]
    bands = (FILT_EMB - 1) // 2
    f = np.linspace(1e-4, bands - 1, bands)[None, :]
    wpos = (2.0 * math.pi) * np.arange(L)[:, None] / L
    z = np.concatenate([t, np.cos(f * wpos), -np.sin(f * wpos)], axis=-1)
    return z.astype(np.float32)


@functools.lru_cache(maxsize=None)
def _filter_deltas():
    d = np.linspace(math.log(FILT_TARGET) / FILT_FAST_DECAY,
                    math.log(FILT_TARGET) / FILT_SLOW_DECAY, HY_W)
    return np.abs(d)[None, :].astype(np.float32)


@functools.lru_cache(maxsize=None)
def _rope_table(n):
    rows = n // GRID_W
    row = np.repeat(np.arange(rows, dtype=np.float64), GRID_W)
    col = np.tile(np.arange(GRID_W, dtype=np.float64), rows)
    nf = D_ROPE // 4
    inv = ROPE_THETA ** (-np.arange(nf, dtype=np.float64) / nf)
    ang = np.stack([row[:, None] * inv, col[:, None] * inv], axis=1)
    cos = np.broadcast_to(np.cos(ang)[:, :, None, :], (n, 2, 2, nf)).reshape(n, D_ROPE)
    sin = np.broadcast_to(np.sin(ang)[:, :, None, :], (n, 2, 2, nf)).reshape(n, D_ROPE)
    return np.concatenate([cos, sin], -1).astype(np.float32)


def _identity_rope_table(n):
    return np.concatenate([np.ones((n, D_ROPE), np.float32), np.zeros((n, D_ROPE), np.float32)], -1)


def _rot_half_cols(w):
    w4 = w.reshape(w.shape[:-1] + (2, 2, D_ROPE // 4))
    return jnp.stack([-w4[..., 1, :], w4[..., 0, :]], axis=-2).reshape(w.shape)


def _mod_kernel(c_ref, w_ref, b_ref, o_ref):
    c = c_ref[...]
    s = c * jax.nn.sigmoid(c)
    o_ref[0] = jnp.dot(s, w_ref[0], preferred_element_type=F32) + b_ref[0]


def _modulation(cvec, ada_w, ada_b):
    depth, d, n = ada_w.shape
    r = cvec.shape[0]
    tn = 512
    return pl.pallas_call(
        _mod_kernel,
        grid=(depth, n // tn),
        in_specs=[pl.BlockSpec((r, d), lambda l, j: (0, 0)),
                  pl.BlockSpec((1, d, tn), lambda l, j: (l, 0, j)),
                  pl.BlockSpec((1, 1, tn), lambda l, j: (l, 0, j))],
        out_specs=pl.BlockSpec((1, r, tn), lambda l, j: (l, 0, j)),
        out_shape=jax.ShapeDtypeStruct((depth, r, n), F32),
        compiler_params=_cparams("arbitrary", "arbitrary"),
        name="adaln_modulation",
    )(cvec, ada_w, ada_b.reshape(depth, 1, n))


def _win_kernel(x_ref, sc_ref, sh_ref, g_ref, w_ref, o_ref, hx_ref):
    @pl.when(pl.program_id(1) == 0)
    def _():
        x = x_ref[...]
        r = lax.rsqrt(jnp.mean(x * x, axis=-1, keepdims=True) + EPS)
        hx = (x * r) * (g_ref[...] * (1.0 + sc_ref[0])) + sh_ref[0]
        hx_ref[...] = hx.astype(BF16)

    tn = o_ref.shape[1]
    col = pl.multiple_of(pl.program_id(1) * tn, tn)
    o_ref[...] = jnp.dot(hx_ref[...], w_ref[:, pl.ds(col, tn)], preferred_element_type=F32).astype(o_ref.dtype)


def _input_proj(x2d, sc, sh, pre_g, w, rows_per_mod, tm, col_lo=0, col_hi=N_PROJ, tn=1024):
    m, d = x2d.shape
    ncols = col_hi - col_lo
    tn = min(tn, ncols)
    per = rows_per_mod // tm
    assert col_lo % ncols == 0
    return pl.pallas_call(
        _win_kernel,
        grid=(m // tm, ncols // tn),
        in_specs=[pl.BlockSpec((tm, d), lambda i, j: (i, 0)),
                  pl.BlockSpec((1, 1, d), lambda i, j: (i // per, 0, 0)),
                  pl.BlockSpec((1, 1, d), lambda i, j: (i // per, 0, 0)),
                  pl.BlockSpec((1, d), lambda i, j: (0, 0)),
                  pl.BlockSpec((d, ncols), lambda i, j: (0, col_lo // ncols), pipeline_mode=pl.Buffered(1))],
        out_specs=pl.BlockSpec((tm, tn), lambda i, j: (i, j)),
        out_shape=jax.ShapeDtypeStruct((m, ncols), BF16),
        scratch_shapes=[pltpu.VMEM((tm, d), BF16)],
        compiler_params=_cparams("arbitrary", "arbitrary"),
        name="prenorm_input_proj",
    )(x2d, sc, sh, pre_g, w)


def _rope_lanes(v, tab):
    t = v * tab
    return t + pltpu.roll(t, D_ROPE, 1)


def _q_kernel(p_ref, g_ref, w_ref, tab_ref, o_ref):
    p = p_ref[...].astype(F32)
    r = lax.rsqrt(jnp.mean(p * p, axis=-1, keepdims=True) + EPS)
    cq = ((p * r) * g_ref[...]).astype(BF16)
    tab = tab_ref[...]
    for h in range(N_HEADS):
        lo = h * D_QK
        res = jnp.dot(cq, w_ref[:, lo:lo + D_QK], preferred_element_type=F32)
        o_ref[:, lo:lo + D_NOPE] = (res[:, :D_NOPE] * Q_SCALE).astype(o_ref.dtype)
        o_ref[:, lo + D_NOPE:lo + D_QK] = (_rope_lanes(res[:, D_NOPE:], tab) * Q_SCALE).astype(o_ref.dtype)


def _q_proj(px, q_g, wq, tab, rows_per_seq, tm):
    m = px.shape[0]
    per = rows_per_seq // tm
    return pl.pallas_call(
        _q_kernel,
        grid=(m // tm,),
        in_specs=[pl.BlockSpec((tm, Q_LORA), lambda i: (i, P_Q // Q_LORA)),
                  pl.BlockSpec((1, Q_LORA), lambda i: (0, 0)),
                  pl.BlockSpec((Q_LORA, N_HEADS * D_QK), lambda i: (0, 0)),
                  pl.BlockSpec((tm, 2 * D_ROPE), lambda i: (i % per, 0))],
        out_specs=pl.BlockSpec((tm, N_HEADS * D_QK), lambda i: (i, 0)),
        out_shape=jax.ShapeDtypeStruct((m, N_HEADS * D_QK), BF16),
        compiler_params=_cparams("arbitrary"),
        name="mla_q_proj",
    )(px, q_g, wq, tab)


def _kv_body(p_ref, g_ref, wk_ref, wvt_ref, tab_ref, k_ref, vt_ref):
    p = p_ref[:, :KV_LORA].astype(F32)
    r = lax.rsqrt(jnp.mean(p * p, axis=-1, keepdims=True) + EPS)
    ckv = ((p * r) * g_ref[...]).astype(BF16)
    kr = _rope_lanes(p_ref[:, KV_LORA:KV_LORA + 2 * D_ROPE].astype(F32), tab_ref[...])
    lane = lax.broadcasted_iota(jnp.int32, kr.shape, 1)
    kr = jnp.where(lane < D_ROPE, kr, 0.0).astype(k_ref.dtype)
    kn = jnp.dot(ckv, wk_ref[...], preferred_element_type=F32).astype(k_ref.dtype)
    for h in range(N_HEADS):
        k_ref[0, :, h * D_QK:h * D_QK + D_NOPE] = kn[:, h * D_NOPE:(h + 1) * D_NOPE]
        k_ref[0, :, h * D_QK + D_NOPE:(h + 1) * D_QK] = kr
    vt = lax.dot_general(wvt_ref[...], ckv, (((1,), (1,)), ((), ())), preferred_element_type=F32)
    vt_ref[0] = vt.astype(vt_ref.dtype)


def _kv_kernel(pc_ref, px_ref, g_ref, wk_ref, wvt_ref, tab_ref, k_ref, vt_ref):
    t = pl.program_id(1)

    @pl.when(t == 0)
    def _():
        _kv_body(pc_ref, g_ref, wk_ref, wvt_ref, tab_ref, k_ref, vt_ref)

    @pl.when(t > 0)
    def _():
        _kv_body(px_ref, g_ref, wk_ref, wvt_ref, tab_ref, k_ref, vt_ref)


def _kv_proj(pc_kv, px, kv_g, wk, wvt, tab_all, bsz, lc, lx):
    tm = lc
    nt = 1 + lx // tm
    per = lx // tm
    kw = 2 * KV_LORA
    return pl.pallas_call(
        _kv_kernel,
        grid=(bsz, nt),
        in_specs=[pl.BlockSpec((tm, kw), lambda b, t: (b, 0)),
                  pl.BlockSpec((tm, kw), lambda b, t: (b * per + jnp.maximum(t - 1, 0), P_KV // kw)),
                  pl.BlockSpec((1, KV_LORA), lambda b, t: (0, 0)),
                  pl.BlockSpec((KV_LORA, N_HEADS * D_NOPE), lambda b, t: (0, 0)),
                  pl.BlockSpec((MLA_W, KV_LORA), lambda b, t: (0, 0)),
                  pl.BlockSpec((tm, 2 * D_ROPE), lambda b, t: (t, 0))],
        out_specs=[pl.BlockSpec((1, tm, N_HEADS * D_QK), lambda b, t: (b, t, 0)),
                   pl.BlockSpec((1, MLA_W, tm), lambda b, t: (b, 0, t))],
        out_shape=[jax.ShapeDtypeStruct((bsz, lc + lx, N_HEADS * D_QK), BF16),
                   jax.ShapeDtypeStruct((bsz, MLA_W, lc + lx), BF16)],
        compiler_params=_cparams("arbitrary", "arbitrary"),
        name="mla_kv_proj",
    )(pc_kv, px, kv_g, wk, wvt, tab_all)


def _attn_kernel(q_ref, k_ref, vt_ref, o_ref, s_ref, *, n_sub):
    rows = q_ref.shape[1] // n_sub

    def scores(t):
        q = q_ref[0, t * rows:(t + 1) * rows, :]
        s_ref[t % 2] = lax.dot_general(k_ref[0], q, (((1,), (1,)), ((), ())), preferred_element_type=F32)

    def finish(t):
        st = s_ref[t % 2]
        m = jnp.max(st, axis=0, keepdims=True)
        pt = jnp.exp2(st - m)
        l = jnp.sum(pt, axis=0, keepdims=True)
        ot = jnp.dot(vt_ref[0], pt.astype(BF16), preferred_element_type=F32)
        o_ref[0, t * rows:(t + 1) * rows, :] = (ot / l).T.astype(o_ref.dtype)

    scores(0)
    for t in range(n_sub):
        if t + 1 < n_sub:
            scores(t + 1)
        finish(t)


def _attention(q, k, vt, lk, tq, n_sub=1):
    bsz, lq, _ = q.shape
    return pl.pallas_call(
        functools.partial(_attn_kernel, n_sub=n_sub),
        grid=(bsz, N_HEADS, lq // tq),
        in_specs=[pl.BlockSpec((1, tq, D_QK), lambda b, h, i: (b, i, h)),
                  pl.BlockSpec((1, lk, D_QK), lambda b, h, i: (b, 0, h)),
                  pl.BlockSpec((1, D_V, lk), lambda b, h, i: (b, h, 0))],
        out_specs=pl.BlockSpec((1, tq, D_V), lambda b, h, i: (b, i, h)),
        out_shape=jax.ShapeDtypeStruct((bsz, lq, MLA_W), BF16),
        scratch_shapes=[pltpu.VMEM((2, lk, tq // n_sub), F32)],
        compiler_params=_cparams("arbitrary", "arbitrary", "arbitrary"),
        name="mla_attention",
    )(q, k, vt)


def _filter_mlp(z_ref, w1_ref, b1_ref, fr_ref, w2_ref, b2_ref):
    hi = lax.Precision.HIGHEST
    fr = fr_ref[...]
    h = jnp.sin(fr * (jnp.dot(z_ref[...], w1_ref[...], precision=hi, preferred_element_type=F32) + b1_ref[...]))
    return jnp.sin(fr * (jnp.dot(h, w2_ref[...], precision=hi, preferred_element_type=F32) + b2_ref[...]))


def _filter_taps(h, z_ref, w3_ref, dl_ref, zero_first):
    taps = jnp.dot(h.astype(BF16), w3_ref[...].astype(BF16), preferred_element_type=F32)
    taps = taps * jnp.exp(-z_ref[:, 0:1] * dl_ref[...])
    if zero_first:
        row = lax.broadcasted_iota(jnp.int32, taps.shape, 0)
        taps = jnp.where(row == 0, 0.0, taps)
    return taps


def _fill_padded(pad_ref, src, n):
    ct = pad_ref.shape[1]
    pad_ref[0:SUB, :] = jnp.zeros((SUB, ct), F32)
    pad_ref[SUB + n:2 * SUB + n, :] = jnp.zeros((SUB, ct), F32)
    pad_ref[SUB:SUB + n, :] = src.astype(F32)


def _short_conv(pad_ref, w_ref, b_ref, r0, rows):
    n = rows + 2 * SUB
    win = pad_ref[r0:r0 + n, :]
    prev = pltpu.roll(win, 1, 0)[SUB:SUB + rows]
    nxt = pltpu.roll(win, n - 1, 0)[SUB:SUB + rows]
    return (b_ref[...] + prev * w_ref[0:1, :] + win[SUB:SUB + rows] * w_ref[1:2, :] + nxt * w_ref[2:3, :])


def _cmul(ur, ui, kr, ki):
    return ur * kr - ui * ki, ur * ki + ui * kr


CONV_CHUNK = 512
MID_UNROLL = 3


def _stage1(u_ref, kf_ref, a_ref):
    n1h, _, ct = u_ref.shape
    for j in range(FFT_N2 // SUB):
        slab = u_ref[:, j * SUB:(j + 1) * SUB, :].reshape(n1h * SUB, ct).astype(BF16)
        r = jnp.dot(kf_ref[...], slab, preferred_element_type=F32)
        a_ref[:, :, j * SUB:(j + 1) * SUB, :] = r.reshape(2, FFT_H1, SUB, ct)


def _stage2(a_ref, rb_ref, i):
    ct = a_ref.shape[-1]
    x = a_ref[:, i].reshape(2 * FFT_N2, ct).astype(BF16)
    return jnp.dot(rb_ref[i], x, preferred_element_type=F32)


def _lfilt_kernel(z_ref, w1_ref, b1_ref, fr_ref, w2_ref, b2_ref, w3f_ref, w3b_ref, dl_ref,
                  kf_ref, rb_ref, o_ref, h_ref, u_ref, a_ref, s_ref):
    ct = u_ref.shape[-1]

    @pl.when(pl.program_id(0) == 0)
    def _():
        h_ref[...] = _filter_mlp(z_ref, w1_ref, b1_ref, fr_ref, w2_ref, b2_ref)

    for w3_ref, back in ((w3f_ref, False), (w3b_ref, True)):
        taps = _filter_taps(h_ref[...], z_ref, w3_ref, dl_ref, zero_first=back)
        u_ref[...] = taps.reshape(u_ref.shape)
        _stage1(u_ref, kf_ref, a_ref)

        def body(i, carry, back=back):
            s = _stage2(a_ref, rb_ref, i).reshape(2, FFT_N2, ct)
            if back:
                o_ref[0, i] = (s_ref[0, i] + s[0]).astype(o_ref.dtype)
                o_ref[1, i] = (s_ref[1, i] - s[1]).astype(o_ref.dtype)
            else:
                s_ref[:, i] = s
            return carry

        lax.fori_loop(0, FFT_H1, body, 0, unroll=MID_UNROLL)


def _long_filter_spectrum(lp, L, ct):
    kron_f, _, rb, _ = _dft_tables()
    z = jnp.asarray(_filter_features(L))
    dl = jnp.asarray(_filter_deltas())
    nct = HY_W // ct
    full = lambda shape: pl.BlockSpec(shape, lambda j: (0,) * len(shape))
    return pl.pallas_call(
        _lfilt_kernel,
        grid=(nct,),
        in_specs=[full((L, FILT_EMB)), full((FILT_EMB, FILT_HIDDEN)), full((1, FILT_HIDDEN)),
                  full((1, FILT_HIDDEN)), full((FILT_HIDDEN, FILT_HIDDEN)), full((1, FILT_HIDDEN)),
                  pl.BlockSpec((FILT_HIDDEN, ct), lambda j: (0, j)),
                  pl.BlockSpec((FILT_HIDDEN, ct), lambda j: (0, j + nct)),
                  pl.BlockSpec((1, ct), lambda j: (0, j)),
                  _const_spec(kron_f.shape), _const_spec(rb.shape)],
        out_specs=pl.BlockSpec((2, FFT_H1, FFT_N2, ct), lambda j: (0, 0, 0, j)),
        out_shape=jax.ShapeDtypeStruct((2, FFT_H1, FFT_N2, HY_W), BF16),
        scratch_shapes=[pltpu.VMEM((L, FILT_HIDDEN), F32),
                        pltpu.VMEM((FFT_N1 // 2, FFT_N2, ct), F32),
                        pltpu.VMEM((2, FFT_H1, FFT_N2, ct), F32),
                        pltpu.VMEM((2, FFT_H1, FFT_N2, ct), F32)],
        compiler_params=_cparams("arbitrary"),
        name="hyena_filter_spectrum",
    )(z, lp["filt_w1"], lp["filt_b1"][None], lp["filt_freq"][None], lp["filt_w2"], lp["filt_b2"][None],
      lp["filt_w3"], lp["filt_w3"], dl, jnp.asarray(kron_f).astype(BF16), jnp.asarray(rb).astype(BF16))


def _lhyena_kernel(x0_ref, x1_ref, v_ref, w0_ref, w1_ref, wv_ref, b0_ref, b1_ref, bv_ref, d_ref,
                   ks_ref, kf_ref, ki_ref, rb_ref, rbt_ref, o_ref, pad_ref, u_ref, a_ref):
    n1h, n2, ct = u_ref.shape
    L = n1h * n2
    nchunk = L // CONV_CHUNK
    cpt = CONV_CHUNK // n2

    _fill_padded(pad_ref, x1_ref[0].reshape(L, ct), L)
    for c in range(nchunk):
        u_ref[c * cpt:(c + 1) * cpt] = _short_conv(pad_ref, w1_ref, b1_ref, c * CONV_CHUNK, CONV_CHUNK).reshape(cpt, n2, ct)
    _fill_padded(pad_ref, v_ref[0].reshape(L, ct), L)
    for c in range(nchunk):
        vv = _short_conv(pad_ref, wv_ref, bv_ref, c * CONV_CHUNK, CONV_CHUNK).reshape(cpt, n2, ct)
        u_ref[c * cpt:(c + 1) * cpt] = u_ref[c * cpt:(c + 1) * cpt] * vv

    _stage1(u_ref, kf_ref, a_ref)

    def body(i, carry):
        uh = _stage2(a_ref, rb_ref, i)
        kr = ks_ref[0, i].astype(F32)
        ki = ks_ref[1, i].astype(F32)
        zr, zi = _cmul(uh[:n2], uh[n2:], kr, ki)
        zz = jnp.concatenate([zr, zi], axis=0).astype(BF16)
        g = jnp.dot(rbt_ref[i], zz, preferred_element_type=F32)
        a_ref[:, i] = g.reshape(2, n2, ct)
        return carry

    lax.fori_loop(0, FFT_H1, body, 0, unroll=MID_UNROLL)

    for j in range(n2 // SUB):
        g = a_ref[:, :, j * SUB:(j + 1) * SUB, :].reshape(2 * FFT_H1 * SUB, ct).astype(BF16)
        y = jnp.dot(ki_ref[...], g, preferred_element_type=F32).reshape(n1h, SUB, ct)
        o_ref[0, :, j * SUB:(j + 1) * SUB, :] = y + u_ref[:, j * SUB:(j + 1) * SUB, :] * d_ref[...]

    _fill_padded(pad_ref, x0_ref[0].reshape(L, ct), L)
    for c in range(nchunk):
        x0 = _short_conv(pad_ref, w0_ref, b0_ref, c * CONV_CHUNK, CONV_CHUNK).reshape(cpt, n2, ct)
        o_ref[0, c * cpt:(c + 1) * cpt] = o_ref[0, c * cpt:(c + 1) * cpt] * x0


def _long_hyena(px, lp, kspec, bsz, L, ct):
    kron_f, kron_i, rb, rbt = _dft_tables()
    n1h = L // FFT_N2
    px4 = px.reshape(bsz, n1h, FFT_N2, N_PROJ)
    nct = HY_W // ct
    c0 = P_HY // ct
    part = lambda k: pl.BlockSpec((1, n1h, FFT_N2, ct), lambda j, b: (b, 0, 0, c0 + k * nct + j))
    wpart = lambda k: pl.BlockSpec((3, ct), lambda j, b: (0, k * nct + j))
    bpart = lambda k: pl.BlockSpec((1, ct), lambda j, b: (0, k * nct + j))
    cb = lp["conv_b"][None]
    out = pl.pallas_call(
        _lhyena_kernel,
        grid=(nct, bsz),
        in_specs=[part(0), part(1), part(2), wpart(0), wpart(1), wpart(2), bpart(0), bpart(1), bpart(2),
                  pl.BlockSpec((1, ct), lambda j, b: (0, j)),
                  pl.BlockSpec((2, FFT_H1, FFT_N2, ct), lambda j, b: (0, 0, 0, j), pipeline_mode=pl.Buffered(1)),
                  _const_spec(kron_f.shape), _const_spec(kron_i.shape),
                  _const_spec(rb.shape), _const_spec(rbt.shape)],
        out_specs=pl.BlockSpec((1, n1h, FFT_N2, ct), lambda j, b: (b, 0, 0, j)),
        out_shape=jax.ShapeDtypeStruct((bsz, n1h, FFT_N2, HY_W), F32),
        scratch_shapes=[pltpu.VMEM((L + 2 * SUB, ct), F32),
                        pltpu.VMEM((n1h, FFT_N2, ct), F32),
                        pltpu.VMEM((2, FFT_H1, FFT_N2, ct), F32)],
        compiler_params=_cparams("arbitrary", "arbitrary"),
        name="hyena_long_conv",
    )(px4, px4, px4, lp["conv_w"], lp["conv_w"], lp["conv_w"], cb, cb, cb, lp["hy_D"][None], kspec,
      jnp.asarray(kron_f).astype(BF16), jnp.asarray(kron_i).astype(BF16),
      jnp.asarray(rb).astype(BF16), jnp.asarray(rbt).astype(BF16))
    return out.reshape(bsz * L, HY_W)


def _sfilt_kernel(z_ref, w1_ref, b1_ref, fr_ref, w2_ref, b2_ref, w3f_ref, w3b_ref, dl_ref, f_ref, o_ref):
    h = _filter_mlp(z_ref, w1_ref, b1_ref, fr_ref, w2_ref, b2_ref)
    nfp = f_ref.shape[0] // 2
    sf = jnp.dot(f_ref[...], _filter_taps(h, z_ref, w3f_ref, dl_ref, False).astype(BF16), preferred_element_type=F32)
    sb = jnp.dot(f_ref[...], _filter_taps(h, z_ref, w3b_ref, dl_ref, True).astype(BF16), preferred_element_type=F32)
    o_ref[0:nfp, :] = sf[:nfp] + sb[:nfp]
    o_ref[nfp:, :] = sf[nfp:] - sb[nfp:]


def _short_filter_spectrum(lp, L, ct):
    fwd, _ = _small_dft_tables(L)
    z = jnp.asarray(_filter_features(L))
    dl = jnp.asarray(_filter_deltas())
    nct = HY_W // ct
    full = lambda shape: pl.BlockSpec(shape, lambda j: (0,) * len(shape))
    return pl.pallas_call(
        _sfilt_kernel,
        grid=(nct,),
        in_specs=[full((L, FILT_EMB)), full((FILT_EMB, FILT_HIDDEN)), full((1, FILT_HIDDEN)),
                  full((1, FILT_HIDDEN)), full((FILT_HIDDEN, FILT_HIDDEN)), full((1, FILT_HIDDEN)),
                  pl.BlockSpec((FILT_HIDDEN, ct), lambda j: (0, j)),
                  pl.BlockSpec((FILT_HIDDEN, ct), lambda j: (0, j + nct)),
                  pl.BlockSpec((1, ct), lambda j: (0, j)),
                  full(fwd.shape)],
        out_specs=pl.BlockSpec((fwd.shape[0], ct), lambda j: (0, j)),
        out_shape=jax.ShapeDtypeStruct((fwd.shape[0], HY_W), F32),
        compiler_params=_cparams("arbitrary"),
        name="hyena_ctx_filter_spectrum",
    )(z, lp["filt_w1"], lp["filt_b1"][None], lp["filt_freq"][None], lp["filt_w2"], lp["filt_b2"][None],
      lp["filt_w3"], lp["filt_w3"], dl, jnp.asarray(fwd).astype(BF16))


def _shyena_kernel(x0_ref, x1_ref, v_ref, w0_ref, w1_ref, wv_ref, b0_ref, b1_ref, bv_ref, d_ref,
                   ks_ref, f_ref, fi_ref, o_ref, pad_ref):
    L = x0_ref.shape[0]
    nfp = f_ref.shape[0] // 2
    _fill_padded(pad_ref, x1_ref[...], L)
    u = _short_conv(pad_ref, w1_ref, b1_ref, 0, L)
    _fill_padded(pad_ref, v_ref[...], L)
    u = u * _short_conv(pad_ref, wv_ref, bv_ref, 0, L)
    uh = jnp.dot(f_ref[...], u.astype(BF16), preferred_element_type=F32)
    zr, zi = _cmul(uh[:nfp], uh[nfp:], ks_ref[0:nfp, :], ks_ref[nfp:, :])
    zz = jnp.concatenate([zr, zi], axis=0).astype(BF16)
    y = jnp.dot(fi_ref[...], zz, preferred_element_type=F32)
    _fill_padded(pad_ref, x0_ref[...], L)
    o_ref[...] = _short_conv(pad_ref, w0_ref, b0_ref, 0, L) * (y + u * d_ref[...])


def _short_hyena(pc, lp, kspec, bsz, L, ct):
    fwd, inv = _small_dft_tables(L)
    nct = HY_W // ct
    c0 = P_HY // ct
    part = lambda k: pl.BlockSpec((L, ct), lambda j, b: (b, c0 + k * nct + j))
    wpart = lambda k: pl.BlockSpec((3, ct), lambda j, b: (0, k * nct + j))
    bpart = lambda k: pl.BlockSpec((1, ct), lambda j, b: (0, k * nct + j))
    full = lambda shape: pl.BlockSpec(shape, lambda j, b: (0,) * len(shape))
    cb = lp["conv_b"][None]
    return pl.pallas_call(
        _shyena_kernel,
        grid=(nct, bsz),
        in_specs=[part(0), part(1), part(2), wpart(0), wpart(1), wpart(2), bpart(0), bpart(1), bpart(2),
                  pl.BlockSpec((1, ct), lambda j, b: (0, j)),
                  pl.BlockSpec((fwd.shape[0], ct), lambda j, b: (0, j)),
                  full(fwd.shape), full(inv.shape)],
        out_specs=pl.BlockSpec((L, ct), lambda j, b: (b, j)),
        out_shape=jax.ShapeDtypeStruct((bsz * L, HY_W), F32),
        scratch_shapes=[pltpu.VMEM((L + 2 * SUB, ct), F32)],
        compiler_params=_cparams("arbitrary", "arbitrary"),
        name="hyena_ctx_conv",
    )(pc, pc, pc, lp["conv_w"], lp["conv_w"], lp["conv_w"], cb, cb, cb, lp["hy_D"][None], kspec,
      jnp.asarray(fwd).astype(BF16), jnp.asarray(inv).astype(BF16))


def _merge_kernel(o_ref, gm_ref, y_ref, gh_ref, x_ref, gx_ref, gmla_ref, ghy_ref, pg_ref, w_ref, out_ref):
    def normed(t, g_ref, gate_ref):
        r = lax.rsqrt(jnp.mean(t * t, axis=-1, keepdims=True) + EPS)
        gate = gate_ref[...].astype(F32)
        return ((t * r) * g_ref[...] * (gate * jax.nn.sigmoid(gate))).astype(BF16)

    a = normed(o_ref[...].astype(F32), gmla_ref, gm_ref)
    b = normed(y_ref[...], ghy_ref, gh_ref)
    z = (jnp.dot(a, w_ref[:MLA_W, :], preferred_element_type=F32)
         + jnp.dot(b, w_ref[MLA_W:, :], preferred_element_type=F32))
    r = lax.rsqrt(jnp.mean(z * z, axis=-1, keepdims=True) + EPS)
    out_ref[...] = x_ref[...] + gx_ref[0] * ((z * r) * pg_ref[...])


def _merge(o, px, yh, x2d, gx, lp, w_out, rows_per_mod, tm):
    m, d = x2d.shape
    per = rows_per_mod // tm
    return pl.pallas_call(
        _merge_kernel,
        grid=(m // tm,),
        in_specs=[pl.BlockSpec((tm, MLA_W), lambda i: (i, 0)),
                  pl.BlockSpec((tm, MLA_W), lambda i: (i, P_GM // MLA_W)),
                  pl.BlockSpec((tm, HY_W), lambda i: (i, 0)),
                  pl.BlockSpec((tm, HY_W), lambda i: (i, P_GH // HY_W)),
                  pl.BlockSpec((tm, d), lambda i: (i, 0)),
                  pl.BlockSpec((1, 1, d), lambda i: (i // per, 0, 0)),
                  pl.BlockSpec((1, MLA_W), lambda i: (0, 0)),
                  pl.BlockSpec((1, HY_W), lambda i: (0, 0)),
                  pl.BlockSpec((1, d), lambda i: (0, 0)),
                  _const_spec((MLA_W + HY_W, d))],
        out_specs=pl.BlockSpec((tm, d), lambda i: (i, 0)),
        out_shape=jax.ShapeDtypeStruct((m, d), F32),
        compiler_params=_cparams("arbitrary"),
        name="branch_merge_out_proj",
    )(o, px, yh, px, x2d, gx, lp["grp_g_mla"][None], lp["grp_g_hy"][None], lp["post_g"][None], w_out)


def _prep_weights(lp):
    w_in = lp["w_in"]
    off_kv = Q_LORA
    off_kr = off_kv + KV_LORA
    off_gm = off_kr + D_ROPE
    off_hy = off_gm + MLA_W
    off_gh = off_hy + 3 * HY_W
    w_in = w_in.astype(BF16)
    w_kr = w_in[:, off_kr:off_gm]
    w_in_p = jnp.concatenate([
        w_in[:, :off_gm], _rot_half_cols(w_kr), jnp.zeros((D_MODEL, P_GM - P_KV - KV_LORA - 2 * D_ROPE), BF16),
        w_in[:, off_gm:]], axis=1)
    wq = lp["w_uq"].reshape(Q_LORA, N_HEADS, D_NOPE + D_ROPE)
    wq_r = wq[..., D_NOPE:]
    wq_cat = jnp.concatenate([wq[..., :D_NOPE], wq_r, _rot_half_cols(wq_r)], axis=-1)
    wq_cat = wq_cat.reshape(Q_LORA, N_HEADS * D_QK).astype(BF16)
    wkv = lp["w_ukv"].reshape(KV_LORA, N_HEADS, D_NOPE + D_V)
    wk = wkv[..., :D_NOPE].reshape(KV_LORA, N_HEADS * D_NOPE).astype(BF16)
    wvt = wkv[..., D_NOPE:].reshape(KV_LORA, MLA_W).T.astype(BF16)
    return w_in_p, wq_cat, wk, wvt, lp["w_out"].astype(BF16)


def _layer(x2d, c2d, mod, lp, bsz, L, Lc, update_ctx):
    d = D_MODEL
    w_in_p, wq_cat, wk, wvt, w_out = _prep_weights(lp)
    sh, sc, gt = mod[:, :d], mod[:, d:2 * d], mod[:, 2 * d:]
    sh_x, sc_x, g_x = (t[:bsz, None, :] for t in (sh, sc, gt))
    sh_c, sc_c, g_c = (t[bsz:bsz + 1, None, :] for t in (sh, sc, gt))
    pre_g = lp["pre_g"][None]

    px = _input_proj(x2d, sc_x, sh_x, pre_g, w_in_p, rows_per_mod=L, tm=1024)
    if update_ctx:
        pc = _input_proj(c2d, sc_c, sh_c, pre_g, w_in_p, rows_per_mod=bsz * Lc, tm=Lc)
        pc_kv = pc[:, P_KV:P_GM]
    else:
        pc_kv = _input_proj(c2d, sc_c, sh_c, pre_g, w_in_p, rows_per_mod=bsz * Lc, tm=Lc,
                            col_lo=P_KV, col_hi=P_GM)

    tab_x = _rope_table(L)
    tab_all = jnp.asarray(np.concatenate([_identity_rope_table(Lc), tab_x], 0))
    k_all, vt_all = _kv_proj(pc_kv, px, lp["kv_norm_g"][None], wk, wvt, tab_all, bsz, Lc, L)
    q_x = _q_proj(px, lp["q_norm_g"][None], wq_cat, jnp.asarray(tab_x), rows_per_seq=L, tm=512)
    o_x = _attention(q_x.reshape(bsz, L, -1), k_all, vt_all, lk=Lc + L, tq=2048, n_sub=4).reshape(bsz * L, MLA_W)

    kspec = _long_filter_spectrum(lp, L, ct=256)
    y_x = _long_hyena(px, lp, kspec, bsz, L, ct=256)
    x_new = _merge(o_x, px, y_x, x2d, g_x, lp, w_out, rows_per_mod=L, tm=512)

    c_new = c2d
    if update_ctx:
        q_c = _q_proj(pc, lp["q_norm_g"][None], wq_cat, jnp.asarray(_identity_rope_table(Lc)),
                      rows_per_seq=Lc, tm=Lc)
        o_c = _attention(q_c.reshape(bsz, Lc, -1), k_all, vt_all, lk=Lc, tq=Lc).reshape(bsz * Lc, MLA_W)
        kspec_c = _short_filter_spectrum(lp, Lc, ct=256)
        y_c = _short_hyena(pc, lp, kspec_c, bsz, Lc, ct=256)
        c_new = _merge(o_c, pc, y_c, c2d, g_c, lp, w_out, rows_per_mod=bsz * Lc, tm=Lc)
    return x_new, c_new


def kernel(x, c, ctx, c_ctx, ada_w, ada_b, pre_g, w_in, q_norm_g, w_uq, kv_norm_g, w_ukv, conv_w, conv_b,
           filt_w1, filt_b1, filt_freq, filt_w2, filt_b2, filt_w3, hy_D, grp_g_mla, grp_g_hy, w_out, post_g):
    bsz, L, d = x.shape
    Lc = ctx.shape[1]
    depth = ada_w.shape[0]
    assert d == D_MODEL and L == (FFT_N1 // 2) * FFT_N2 and bsz + 1 <= SUB
    cvec = jnp.concatenate([c, c_ctx[None], jnp.zeros((SUB - bsz - 1, d), F32)], axis=0)
    mod = _modulation(cvec, ada_w, ada_b)
    params = dict(pre_g=pre_g, w_in=w_in, q_norm_g=q_norm_g, w_uq=w_uq, kv_norm_g=kv_norm_g, w_ukv=w_ukv,
                  conv_w=conv_w, conv_b=conv_b, filt_w1=filt_w1, filt_b1=filt_b1, filt_freq=filt_freq,
                  filt_w2=filt_w2, filt_b2=filt_b2, filt_w3=filt_w3, hy_D=hy_D, grp_g_mla=grp_g_mla,
                  grp_g_hy=grp_g_hy, w_out=w_out, post_g=post_g)
    x2d = x.reshape(bsz * L, d)
    c2d = ctx.reshape(bsz * Lc, d)
    for l in range(depth):
        lp = {k: v[l] for k, v in params.items()}
        x2d, c2d = _layer(x2d, c2d, mod[l], lp, bsz, L, Lc, update_ctx=(l < depth - 1))
    return x2d.reshape(bsz, L, d)
```

```python
import functools
import math

import numpy as np
import jax
import jax.numpy as jnp
from jax import lax
from jax.experimental import pallas as pl
from jax.experimental.pallas import tpu as pltpu

F32 = jnp.float32
BF16 = jnp.bfloat16

D_MODEL = 2048
GRID_W = 64
N_HEADS = 8
D_NOPE = 128
D_ROPE = 64
D_V = 128
MLA_W = N_HEADS * D_V
Q_LORA = 512
KV_LORA = 256
ROPE_THETA = 10000.0
SCALE = (D_NOPE + D_ROPE) ** -0.5
Q_SCALE = SCALE * math.log2(math.e)
HY_W = 1024
FILT_EMB = 33
FILT_HIDDEN = 64
FILT_TARGET = 1e-2
FILT_FAST_DECAY = 0.3
FILT_SLOW_DECAY = 1.5
EPS = 1e-6

P_Q = 0
P_KV = 512
P_GM = 1024
P_HY = 2048
P_GH = 5120
N_PROJ = 6144
D_QK = 256

VMEM_LIMIT = 58 * 1024 * 1024

FFT_N1 = 64
FFT_N2 = 128
FFT_H1 = FFT_N1 // 2 + 1
SUB = 8


def _cparams(*sem):
    return pltpu.CompilerParams(dimension_semantics=sem, vmem_limit_bytes=VMEM_LIMIT)


def _const_spec(shape):
    nd = len(shape)
    return pl.BlockSpec(shape, lambda *_: (0,) * nd, pipeline_mode=pl.Buffered(1))


@functools.lru_cache(maxsize=None)
def _dft_tables():
    n1 = np.arange(FFT_N1 // 2)
    k1 = np.arange(FFT_H1)
    eye = np.eye(SUB)
    ang = 2 * np.pi * np.outer(k1, n1) / FFT_N1
    kron_f = np.concatenate([np.kron(np.cos(ang), eye), np.kron(-np.sin(ang), eye)], 0)
    ck = np.where((k1 == 0) | (k1 == FFT_N1 // 2), 1.0, 2.0)
    angi = 2 * np.pi * np.outer(n1, k1) / FFT_N1
    kron_i = np.concatenate([np.kron(np.cos(angi) * ck, eye),
                             np.kron((-np.sin(angi) * ck)[:, 1:FFT_H1 - 1], eye)], 1) / (FFT_N1 * FFT_N2)
    n2 = np.arange(FFT_N2)
    kk = FFT_N1 * np.arange(FFT_N2)[None, :, None] + k1[:, None, None]
    a = 2 * np.pi * kk * n2[None, None, :] / (FFT_N1 * FFT_N2)
    mr, mi = np.cos(a), -np.sin(a)
    rb = np.concatenate([np.concatenate([mr, -mi], 2), np.concatenate([mi, mr], 2)], 1)
    rbt = np.transpose(rb, (0, 2, 1))
    return (kron_f.astype(np.float32), kron_i.astype(np.float32),
            rb.astype(np.float32), rbt.astype(np.float32))


@functools.lru_cache(maxsize=None)
def _small_dft_tables(L):
    nf = L + 1
    nfp = -(-nf // SUB) * SUB
    n = np.arange(L)
    k = np.arange(nf)
    ang = 2 * np.pi * np.outer(k, n) / (2 * L)
    fwd = np.zeros((2 * nfp, L))
    fwd[:nf] = np.cos(ang)
    fwd[nfp:nfp + nf] = -np.sin(ang)
    ck = np.where((k == 0) | (k == L), 1.0, 2.0)
    inv = np.zeros((L, 2 * nfp))
    inv[:, :nf] = np.cos(ang).T * ck / (2 * L)
    inv[:, nfp:nfp + nf] = -np.sin(ang).T * ck / (2 * L)
    return fwd.astype(np.float32), inv.astype(np.float32)


@functools.lru_cache(maxsize=None)
def _filter_features(L):
    t = np.linspace(0.0, 1.0, L)[:, None]
    bands = (FILT_EMB - 1) // 2
    f = np.linspace(1e-4, bands - 1, bands)[None, :]
    wpos = (2.0 * math.pi) * np.arange(L)[:, None] / L
    z = np.concatenate([t, np.cos(f * wpos), -np.sin(f * wpos)], axis=-1)
    return z.astype(np.float32)


@functools.lru_cache(maxsize=None)
def _filter_deltas():
    d = np.linspace(math.log(FILT_TARGET) / FILT_FAST_DECAY,
                    math.log(FILT_TARGET) / FILT_SLOW_DECAY, HY_W)
    return np.abs(d)[None, :].astype(np.float32)


@functools.lru_cache(maxsize=None)
def _rope_table(n):
    rows = n // GRID_W
    row = np.repeat(np.arange(rows, dtype=np.float64), GRID_W)
    col = np.tile(np.arange(GRID_W, dtype=np.float64), rows)
    nf = D_ROPE // 4
    inv = ROPE_THETA ** (-np.arange(nf, dtype=np.float64) / nf)
    ang = np.stack([row[:, None] * inv, col[:, None] * inv], axis=1)
    cos = np.broadcast_to(np.cos(ang)[:, :, None, :], (n, 2, 2, nf)).reshape(n, D_ROPE)
    sin = np.broadcast_to(np.sin(ang)[:, :, None, :], (n, 2, 2, nf)).reshape(n, D_ROPE)
    return np.concatenate([cos, sin], -1).astype(np.float32)


def _identity_rope_table(n):
    return np.concatenate([np.ones((n, D_ROPE), np.float32), np.zeros((n, D_ROPE), np.float32)], -1)


def _rot_half_cols(w):
    w4 = w.reshape(w.shape[:-1] + (2, 2, D_ROPE // 4))
    return jnp.stack([-w4[..., 1, :], w4[..., 0, :]], axis=-2).reshape(w.shape)


def _mod_kernel(c_ref, w_ref, b_ref, o_ref):
    c = c_ref[...]
    s = c * jax.nn.sigmoid(c)
    o_ref[0] = jnp.dot(s, w_ref[0], preferred_element_type=F32) + b_ref[0]


def _modulation(cvec, ada_w, ada_b):
    depth, d, n = ada_w.shape
    r = cvec.shape[0]
    tn = 512
    return pl.pallas_call(
        _mod_kernel,
        grid=(depth, n // tn),
        in_specs=[pl.BlockSpec((r, d), lambda l, j: (0, 0)),
                  pl.BlockSpec((1, d, tn), lambda l, j: (l, 0, j)),
                  pl.BlockSpec((1, 1, tn), lambda l, j: (l, 0, j))],
        out_specs=pl.BlockSpec((1, r, tn), lambda l, j: (l, 0, j)),
        out_shape=jax.ShapeDtypeStruct((depth, r, n), F32),
        compiler_params=_cparams("arbitrary", "arbitrary"),
        name="adaln_modulation",
    )(cvec, ada_w, ada_b.reshape(depth, 1, n))


def _win_kernel(x_ref, sc_ref, sh_ref, g_ref, w_ref, o_ref, hx_ref):
    @pl.when(pl.program_id(1) == 0)
    def _():
        x = x_ref[...]
        r = lax.rsqrt(jnp.mean(x * x, axis=-1, keepdims=True) + EPS)
        hx = (x * r) * (g_ref[...] * (1.0 + sc_ref[0])) + sh_ref[0]
        hx_ref[...] = hx.astype(BF16)

    tn = o_ref.shape[1]
    col = pl.multiple_of(pl.program_id(1) * tn, tn)
    o_ref[...] = jnp.dot(hx_ref[...], w_ref[:, pl.ds(col, tn)], preferred_element_type=F32).astype(o_ref.dtype)


def _input_proj(x2d, sc, sh, pre_g, w, rows_per_mod, tm, col_lo=0, col_hi=N_PROJ, tn=1024):
    m, d = x2d.shape
    ncols = col_hi - col_lo
    tn = min(tn, ncols)
    per = rows_per_mod // tm
    assert col_lo % ncols == 0
    return pl.pallas_call(
        _win_kernel,
        grid=(m // tm, ncols // tn),
        in_specs=[pl.BlockSpec((tm, d), lambda i, j: (i, 0)),
                  pl.BlockSpec((1, 1, d), lambda i, j: (i // per, 0, 0)),
                  pl.BlockSpec((1, 1, d), lambda i, j: (i // per, 0, 0)),
                  pl.BlockSpec((1, d), lambda i, j: (0, 0)),
                  pl.BlockSpec((d, ncols), lambda i, j: (0, col_lo // ncols), pipeline_mode=pl.Buffered(1))],
        out_specs=pl.BlockSpec((tm, tn), lambda i, j: (i, j)),
        out_shape=jax.ShapeDtypeStruct((m, ncols), BF16),
        scratch_shapes=[pltpu.VMEM((tm, d), BF16)],
        compiler_params=_cparams("arbitrary", "arbitrary"),
        name="prenorm_input_proj",
    )(x2d, sc, sh, pre_g, w)


def _rope_lanes(v, tab):
    t = v * tab
    return t + pltpu.roll(t, D_ROPE, 1)


def _q_kernel(p_ref, g_ref, w_ref, tab_ref, o_ref):
    p = p_ref[...].astype(F32)
    r = lax.rsqrt(jnp.mean(p * p, axis=-1, keepdims=True) + EPS)
    cq = ((p * r) * g_ref[...]).astype(BF16)
    tab = tab_ref[...]
    for h in range(N_HEADS):
        lo = h * D_QK
        res = jnp.dot(cq, w_ref[:, lo:lo + D_QK], preferred_element_type=F32)
        o_ref[:, lo:lo + D_NOPE] = (res[:, :D_NOPE] * Q_SCALE).astype(o_ref.dtype)
        o_ref[:, lo + D_NOPE:lo + D_QK] = (_rope_lanes(res[:, D_NOPE:], tab) * Q_SCALE).astype(o_ref.dtype)


def _q_proj(px, q_g, wq, tab, rows_per_seq, tm):
    m = px.shape[0]
    per = rows_per_seq // tm
    return pl.pallas_call(
        _q_kernel,
        grid=(m // tm,),
        in_specs=[pl.BlockSpec((tm, Q_LORA), lambda i: (i, P_Q // Q_LORA)),
                  pl.BlockSpec((1, Q_LORA), lambda i: (0, 0)),
                  pl.BlockSpec((Q_LORA, N_HEADS * D_QK), lambda i: (0, 0)),
                  pl.BlockSpec((tm, 2 * D_ROPE), lambda i: (i % per, 0))],
        out_specs=pl.BlockSpec((tm, N_HEADS * D_QK), lambda i: (i, 0)),
        out_shape=jax.ShapeDtypeStruct((m, N_HEADS * D_QK), BF16),
        compiler_params=_cparams("arbitrary"),
        name="mla_q_proj",
    )(px, q_g, wq, tab)


def _kv_kernel(p_ref, g_ref, wk_ref, wvt_ref, tab_ref, k_ref, vt_ref):
    p = p_ref[:, :KV_LORA].astype(F32)
    r = lax.rsqrt(jnp.mean(p * p, axis=-1, keepdims=True) + EPS)
    ckv = ((p * r) * g_ref[...]).astype(BF16)
    kr = _rope_lanes(p_ref[:, KV_LORA:KV_LORA + 2 * D_ROPE].astype(F32), tab_ref[...])
    lane = lax.broadcasted_iota(jnp.int32, kr.shape, 1)
    kr = jnp.where(lane < D_ROPE, kr, 0.0).astype(k_ref.dtype)
    kn = jnp.dot(ckv, wk_ref[...], preferred_element_type=F32).astype(k_ref.dtype)
    for h in range(N_HEADS):
        k_ref[0, :, h * D_QK:h * D_QK + D_NOPE] = kn[:, h * D_NOPE:(h + 1) * D_NOPE]
        k_ref[0, :, h * D_QK + D_NOPE:(h + 1) * D_QK] = kr
    vt = lax.dot_general(wvt_ref[...], ckv, (((1,), (1,)), ((), ())), preferred_element_type=F32)
    vt_ref[0] = vt.astype(vt_ref.dtype)


def _kv_proj(p, col_blk, kv_g, wk, wvt, tab, bsz, ltok, tm):
    per = ltok // tm
    kw = 2 * KV_LORA
    return pl.pallas_call(
        _kv_kernel,
        grid=(bsz, per),
        in_specs=[pl.BlockSpec((tm, kw), lambda b, t: (b * per + t, col_blk)),
                  pl.BlockSpec((1, KV_LORA), lambda b, t: (0, 0)),
                  pl.BlockSpec((KV_LORA, N_HEADS * D_NOPE), lambda b, t: (0, 0)),
                  pl.BlockSpec((MLA_W, KV_LORA), lambda b, t: (0, 0)),
                  pl.BlockSpec((tm, 2 * D_ROPE), lambda b, t: (t, 0))],
        out_specs=[pl.BlockSpec((1, tm, N_HEADS * D_QK), lambda b, t: (b, t, 0)),
                   pl.BlockSpec((1, MLA_W, tm), lambda b, t: (b, 0, t))],
        out_shape=[jax.ShapeDtypeStruct((bsz, ltok, N_HEADS * D_QK), BF16),
                   jax.ShapeDtypeStruct((bsz, MLA_W, ltok), BF16)],
        compiler_params=_cparams("arbitrary", "arbitrary"),
        name="mla_kv_proj",
    )(p, kv_g, wk, wvt, tab)


def _attn_kernel(q_ref, *refs, n_sub, n_kv):
    k_refs, vt_refs = refs[0:2 * n_kv:2], refs[1:2 * n_kv:2]
    o_ref, s_ref = refs[2 * n_kv:]
    rows = q_ref.shape[1] // n_sub
    offs = [0]
    for k_ref in k_refs:
        offs.append(offs[-1] + k_ref.shape[1])

    def scores(t):
        q = q_ref[0, t * rows:(t + 1) * rows, :]
        for i, k_ref in enumerate(k_refs):
            s_ref[t % 2, offs[i]:offs[i + 1]] = lax.dot_general(
                k_ref[0], q, (((1,), (1,)), ((), ())), preferred_element_type=F32)

    def finish(t):
        st = s_ref[t % 2]
        m = jnp.max(st, axis=0, keepdims=True)
        pt = jnp.exp2(st - m)
        l = jnp.sum(pt, axis=0, keepdims=True)
        pt = pt.astype(BF16)
        ot = sum(jnp.dot(vt_ref[0], pt[offs[i]:offs[i + 1]], preferred_element_type=F32)
                 for i, vt_ref in enumerate(vt_refs))
        o_ref[0, t * rows:(t + 1) * rows, :] = (ot / l).T.astype(o_ref.dtype)

    scores(0)
    for t in range(n_sub):
        if t + 1 < n_sub:
            scores(t + 1)
        finish(t)


def _attention(q, kv_sets, tq, n_sub=1):
    bsz, lq, _ = q.shape
    in_specs = [pl.BlockSpec((1, tq, D_QK), lambda b, h, i: (b, i, h))]
    args = [q]
    for k, vt in kv_sets:
        lk = k.shape[1]
        in_specs += [pl.BlockSpec((1, lk, D_QK), lambda b, h, i: (b, 0, h)),
                     pl.BlockSpec((1, D_V, lk), lambda b, h, i: (b, h, 0))]
        args += [k, vt]
    lk_all = sum(k.shape[1] for k, _ in kv_sets)
    return pl.pallas_call(
        functools.partial(_attn_kernel, n_sub=n_sub, n_kv=len(kv_sets)),
        grid=(bsz, N_HEADS, lq // tq),
        in_specs=in_specs,
        out_specs=pl.BlockSpec((1, tq, D_V), lambda b, h, i: (b, i, h)),
        out_shape=jax.ShapeDtypeStruct((bsz, lq, MLA_W), BF16),
        scratch_shapes=[pltpu.VMEM((2, lk_all, tq // n_sub), F32)],
        compiler_params=_cparams("arbitrary", "arbitrary", "arbitrary"),
        name="mla_attention",
    )(*args)


def _filter_mlp(z_ref, w1_ref, b1_ref, fr_ref, w2_ref, b2_ref):
    hi = lax.Precision.HIGHEST
    fr = fr_ref[...]
    h = jnp.sin(fr * (jnp.dot(z_ref[...], w1_ref[...], precision=hi, preferred_element_type=F32) + b1_ref[...]))
    return jnp.sin(fr * (jnp.dot(h, w2_ref[...], precision=hi, preferred_element_type=F32) + b2_ref[...]))


def _filter_taps(h, z_ref, w3_ref, dl_ref, zero_first):
    taps = jnp.dot(h.astype(BF16), w3_ref[...].astype(BF16), preferred_element_type=F32)
    taps = taps * jnp.exp(-z_ref[:, 0:1] * dl_ref[...])
    if zero_first:
        row = lax.broadcasted_iota(jnp.int32, taps.shape, 0)
        taps = jnp.where(row == 0, 0.0, taps)
    return taps


def _fill_padded(pad_ref, src, n):
    ct = pad_ref.shape[1]
    pad_ref[0:SUB, :] = jnp.zeros((SUB, ct), F32)
    pad_ref[SUB + n:2 * SUB + n, :] = jnp.zeros((SUB, ct), F32)
    pad_ref[SUB:SUB + n, :] = src.astype(F32)


def _short_conv(pad_ref, w_ref, b_ref, r0, rows):
    n = rows + 2 * SUB
    win = pad_ref[r0:r0 + n, :]
    prev = pltpu.roll(win, 1, 0)[SUB:SUB + rows]
    nxt = pltpu.roll(win, n - 1, 0)[SUB:SUB + rows]
    return (b_ref[...] + prev * w_ref[0:1, :] + win[SUB:SUB + rows] * w_ref[1:2, :] + nxt * w_ref[2:3, :])


def _cmul(ur, ui, kr, ki):
    return ur * kr - ui * ki, ur * ki + ui * kr


CONV_CHUNK = 512
MID_UNROLL = 11


def _stage1(u_ref, kf_ref, a_ref):
    n1h, _, ct = u_ref.shape
    for j in range(FFT_N2 // SUB):
        slab = u_ref[:, j * SUB:(j + 1) * SUB, :].reshape(n1h * SUB, ct).astype(BF16)
        r = jnp.dot(kf_ref[...], slab, preferred_element_type=F32)
        a_ref[:, :, j * SUB:(j + 1) * SUB, :] = r.reshape(2, FFT_H1, SUB, ct)


def _stage2(a_ref, rb_ref, i):
    ct = a_ref.shape[-1]
    x = a_ref[:, i].reshape(2 * FFT_N2, ct).astype(BF16)
    return jnp.dot(rb_ref[i], x, preferred_element_type=F32)


def _lfilt_kernel(z_ref, w1_ref, b1_ref, fr_ref, w2_ref, b2_ref, w3f_ref, w3b_ref, dl_ref,
                  kf_ref, rb_ref, o_ref, h_ref, u_ref, a_ref, s_ref):
    ct = u_ref.shape[-1]

    @pl.when(pl.program_id(0) == 0)
    def _():
        h_ref[...] = _filter_mlp(z_ref, w1_ref, b1_ref, fr_ref, w2_ref, b2_ref)

    for w3_ref, back in ((w3f_ref, False), (w3b_ref, True)):
        taps = _filter_taps(h_ref[...], z_ref, w3_ref, dl_ref, zero_first=back)
        u_ref[...] = taps.reshape(u_ref.shape)
        _stage1(u_ref, kf_ref, a_ref)

        def body(i, carry, back=back):
            s = _stage2(a_ref, rb_ref, i).reshape(2, FFT_N2, ct)
            if back:
                o_ref[0, i] = (s_ref[0, i] + s[0]).astype(o_ref.dtype)
                o_ref[1, i] = (s_ref[1, i] - s[1]).astype(o_ref.dtype)
            else:
                s_ref[:, i] = s
            return carry

        lax.fori_loop(0, FFT_H1, body, 0, unroll=MID_UNROLL)


def _long_filter_spectrum(lp, L, ct):
    kron_f, _, rb, _ = _dft_tables()
    z = jnp.asarray(_filter_features(L))
    dl = jnp.asarray(_filter_deltas())
    nct = HY_W // ct
    full = lambda shape: pl.BlockSpec(shape, lambda j: (0,) * len(shape))
    return pl.pallas_call(
        _lfilt_kernel,
        grid=(nct,),
        in_specs=[full((L, FILT_EMB)), full((FILT_EMB, FILT_HIDDEN)), full((1, FILT_HIDDEN)),
                  full((1, FILT_HIDDEN)), full((FILT_HIDDEN, FILT_HIDDEN)), full((1, FILT_HIDDEN)),
                  pl.BlockSpec((FILT_HIDDEN, ct), lambda j: (0, j)),
                  pl.BlockSpec((FILT_HIDDEN, ct), lambda j: (0, j + nct)),
                  pl.BlockSpec((1, ct), lambda j: (0, j)),
                  _const_spec(kron_f.shape), _const_spec(rb.shape)],
        out_specs=pl.BlockSpec((2, FFT_H1, FFT_N2, ct), lambda j: (0, 0, 0, j)),
        out_shape=jax.ShapeDtypeStruct((2, FFT_H1, FFT_N2, HY_W), BF16),
        scratch_shapes=[pltpu.VMEM((L, FILT_HIDDEN), F32),
                        pltpu.VMEM((FFT_N1 // 2, FFT_N2, ct), F32),
                        pltpu.VMEM((2, FFT_H1, FFT_N2, ct), F32),
                        pltpu.VMEM((2, FFT_H1, FFT_N2, ct), F32)],
        compiler_params=_cparams("arbitrary"),
        name="hyena_filter_spectrum",
    )(z, lp["filt_w1"], lp["filt_b1"][None], lp["filt_freq"][None], lp["filt_w2"], lp["filt_b2"][None],
      lp["filt_w3"], lp["filt_w3"], dl, jnp.asarray(kron_f).astype(BF16), jnp.asarray(rb).astype(BF16))


def _lhyena_kernel(x0_ref, x1_ref, v_ref, w0_ref, w1_ref, wv_ref, b0_ref, b1_ref, bv_ref, d_ref,
                   ks_ref, kf_ref, ki_ref, rb_ref, rbt_ref, o_ref, pad_ref, u_ref, a_ref):
    n1h, n2, ct = u_ref.shape
    L = n1h * n2
    nchunk = L // CONV_CHUNK
    cpt = CONV_CHUNK // n2

    _fill_padded(pad_ref, x1_ref[0].reshape(L, ct), L)
    for c in range(nchunk):
        u_ref[c * cpt:(c + 1) * cpt] = _short_conv(pad_ref, w1_ref, b1_ref, c * CONV_CHUNK, CONV_CHUNK).reshape(cpt, n2, ct)
    _fill_padded(pad_ref, v_ref[0].reshape(L, ct), L)
    for c in range(nchunk):
        vv = _short_conv(pad_ref, wv_ref, bv_ref, c * CONV_CHUNK, CONV_CHUNK).reshape(cpt, n2, ct)
        u_ref[c * cpt:(c + 1) * cpt] = u_ref[c * cpt:(c + 1) * cpt] * vv

    _stage1(u_ref, kf_ref, a_ref)

    def body(i, carry):
        uh = _stage2(a_ref, rb_ref, i)
        kr = ks_ref[0, i].astype(F32)
        ki = ks_ref[1, i].astype(F32)
        zr, zi = _cmul(uh[:n2], uh[n2:], kr, ki)
        zz = jnp.concatenate([zr, zi], axis=0).astype(BF16)
        g = jnp.dot(rbt_ref[i], zz, preferred_element_type=F32)
        a_ref[:, i] = g.reshape(2, n2, ct)
        return carry

    lax.fori_loop(0, FFT_H1, body, 0, unroll=MID_UNROLL)

    for j in range(n2 // SUB):
        g = jnp.concatenate([
            a_ref[0, :, j * SUB:(j + 1) * SUB, :].reshape(FFT_H1 * SUB, ct),
            a_ref[1, 1:FFT_H1 - 1, j * SUB:(j + 1) * SUB, :].reshape((FFT_H1 - 2) * SUB, ct)], axis=0).astype(BF16)
        y = jnp.dot(ki_ref[...], g, preferred_element_type=F32).reshape(n1h, SUB, ct)
        o_ref[0, :, j * SUB:(j + 1) * SUB, :] = y + u_ref[:, j * SUB:(j + 1) * SUB, :] * d_ref[...]

    _fill_padded(pad_ref, x0_ref[0].reshape(L, ct), L)
    for c in range(nchunk):
        x0 = _short_conv(pad_ref, w0_ref, b0_ref, c * CONV_CHUNK, CONV_CHUNK).reshape(cpt, n2, ct)
        o_ref[0, c * cpt:(c + 1) * cpt] = o_ref[0, c * cpt:(c + 1) * cpt] * x0


def _long_hyena(px, lp, kspec, bsz, L, ct):
    kron_f, kron_i, rb, rbt = _dft_tables()
    n1h = L // FFT_N2
    px4 = px.reshape(bsz, n1h, FFT_N2, N_PROJ)
    nct = HY_W // ct
    c0 = P_HY // ct
    part = lambda k: pl.BlockSpec((1, n1h, FFT_N2, ct), lambda j, b: (b, 0, 0, c0 + k * nct + j))
    wpart = lambda k: pl.BlockSpec((3, ct), lambda j, b: (0, k * nct + j))
    bpart = lambda k: pl.BlockSpec((1, ct), lambda j, b: (0, k * nct + j))
    cb = lp["conv_b"][None]
    out = pl.pallas_call(
        _lhyena_kernel,
        grid=(nct, bsz),
        in_specs=[part(0), part(1), part(2), wpart(0), wpart(1), wpart(2), bpart(0), bpart(1), bpart(2),
                  pl.BlockSpec((1, ct), lambda j, b: (0, j)),
                  pl.BlockSpec((2, FFT_H1, FFT_N2, ct), lambda j, b: (0, 0, 0, j), pipeline_mode=pl.Buffered(1)),
                  _const_spec(kron_f.shape), _const_spec(kron_i.shape),
                  _const_spec(rb.shape), _const_spec(rbt.shape)],
        out_specs=pl.BlockSpec((1, n1h, FFT_N2, ct), lambda j, b: (b, 0, 0, j)),
        out_shape=jax.ShapeDtypeStruct((bsz, n1h, FFT_N2, HY_W), F32),
        scratch_shapes=[pltpu.VMEM((L + 2 * SUB, ct), F32),
                        pltpu.VMEM((n1h, FFT_N2, ct), F32),
                        pltpu.VMEM((2, FFT_H1, FFT_N2, ct), F32)],
        compiler_params=_cparams("arbitrary", "arbitrary"),
        name="hyena_long_conv",
    )(px4, px4, px4, lp["conv_w"], lp["conv_w"], lp["conv_w"], cb, cb, cb, lp["hy_D"][None], kspec,
      jnp.asarray(kron_f).astype(BF16), jnp.asarray(kron_i).astype(BF16),
      jnp.asarray(rb).astype(BF16), jnp.asarray(rbt).astype(BF16))
    return out.reshape(bsz * L, HY_W)


def _sfilt_kernel(z_ref, w1_ref, b1_ref, fr_ref, w2_ref, b2_ref, w3f_ref, w3b_ref, dl_ref, f_ref, o_ref):
    h = _filter_mlp(z_ref, w1_ref, b1_ref, fr_ref, w2_ref, b2_ref)
    nfp = f_ref.shape[0] // 2
    sf = jnp.dot(f_ref[...], _filter_taps(h, z_ref, w3f_ref, dl_ref, False).astype(BF16), preferred_element_type=F32)
    sb = jnp.dot(f_ref[...], _filter_taps(h, z_ref, w3b_ref, dl_ref, True).astype(BF16), preferred_element_type=F32)
    o_ref[0:nfp, :] = sf[:nfp] + sb[:nfp]
    o_ref[nfp:, :] = sf[nfp:] - sb[nfp:]


def _short_filter_spectrum(lp, L, ct):
    fwd, _ = _small_dft_tables(L)
    z = jnp.asarray(_filter_features(L))
    dl = jnp.asarray(_filter_deltas())
    nct = HY_W // ct
    full = lambda shape: pl.BlockSpec(shape, lambda j: (0,) * len(shape))
    return pl.pallas_call(
        _sfilt_kernel,
        grid=(nct,),
        in_specs=[full((L, FILT_EMB)), full((FILT_EMB, FILT_HIDDEN)), full((1, FILT_HIDDEN)),
                  full((1, FILT_HIDDEN)), full((FILT_HIDDEN, FILT_HIDDEN)), full((1, FILT_HIDDEN)),
                  pl.BlockSpec((FILT_HIDDEN, ct), lambda j: (0, j)),
                  pl.BlockSpec((FILT_HIDDEN, ct), lambda j: (0, j + nct)),
                  pl.BlockSpec((1, ct), lambda j: (0, j)),
                  full(fwd.shape)],
        out_specs=pl.BlockSpec((fwd.shape[0], ct), lambda j: (0, j)),
        out_shape=jax.ShapeDtypeStruct((fwd.shape[0], HY_W), F32),
        compiler_params=_cparams("arbitrary"),
        name="hyena_ctx_filter_spectrum",
    )(z, lp["filt_w1"], lp["filt_b1"][None], lp["filt_freq"][None], lp["filt_w2"], lp["filt_b2"][None],
      lp["filt_w3"], lp["filt_w3"], dl, jnp.asarray(fwd).astype(BF16))


def _shyena_kernel(x0_ref, x1_ref, v_ref, w0_ref, w1_ref, wv_ref, b0_ref, b1_ref, bv_ref, d_ref,
                   ks_ref, f_ref, fi_ref, o_ref, pad_ref):
    L = x0_ref.shape[0]
    nfp = f_ref.shape[0] // 2
    _fill_padded(pad_ref, x1_ref[...], L)
    u = _short_conv(pad_ref, w1_ref, b1_ref, 0, L)
    _fill_padded(pad_ref, v_ref[...], L)
    u = u * _short_conv(pad_ref, wv_ref, bv_ref, 0, L)
    uh = jnp.dot(f_ref[...], u.astype(BF16), preferred_element_type=F32)
    zr, zi = _cmul(uh[:nfp], uh[nfp:], ks_ref[0:nfp, :], ks_ref[nfp:, :])
    zz = jnp.concatenate([zr, zi], axis=0).astype(BF16)
    y = jnp.dot(fi_ref[...], zz, preferred_element_type=F32)
    _fill_padded(pad_ref, x0_ref[...], L)
    o_ref[...] = _short_conv(pad_ref, w0_ref, b0_ref, 0, L) * (y + u * d_ref[...])


def _short_hyena(pc, lp, kspec, bsz, L, ct):
    fwd, inv = _small_dft_tables(L)
    nct = HY_W // ct
    c0 = P_HY // ct
    part = lambda k: pl.BlockSpec((L, ct), lambda j, b: (b, c0 + k * nct + j))
    wpart = lambda k: pl.BlockSpec((3, ct), lambda j, b: (0, k * nct + j))
    bpart = lambda k: pl.BlockSpec((1, ct), lambda j, b: (0, k * nct + j))
    full = lambda shape: pl.BlockSpec(shape, lambda j, b: (0,) * len(shape))
    cb = lp["conv_b"][None]
    return pl.pallas_call(
        _shyena_kernel,
        grid=(nct, bsz),
        in_specs=[part(0), part(1), part(2), wpart(0), wpart(1), wpart(2), bpart(0), bpart(1), bpart(2),
                  pl.BlockSpec((1, ct), lambda j, b: (0, j)),
                  pl.BlockSpec((fwd.shape[0], ct), lambda j, b: (0, j)),
                  full(fwd.shape), full(inv.shape)],
        out_specs=pl.BlockSpec((L, ct), lambda j, b: (b, j)),
        out_shape=jax.ShapeDtypeStruct((bsz * L, HY_W), F32),
        scratch_shapes=[pltpu.VMEM((L + 2 * SUB, ct), F32)],
        compiler_params=_cparams("arbitrary", "arbitrary"),
        name="hyena_ctx_conv",
    )(pc, pc, pc, lp["conv_w"], lp["conv_w"], lp["conv_w"], cb, cb, cb, lp["hy_D"][None], kspec,
      jnp.asarray(fwd).astype(BF16), jnp.asarray(inv).astype(BF16))


def _merge_kernel(o_ref, gm_ref, y_ref, gh_ref, x_ref, gx_ref, gmla_ref, ghy_ref, pg_ref, w_ref, out_ref):
    def normed(t, g_ref, gate_ref):
        r = lax.rsqrt(jnp.mean(t * t, axis=-1, keepdims=True) + EPS)
        gate = gate_ref[...].astype(F32)
        return ((t * r) * g_ref[...] * (gate * jax.nn.sigmoid(gate))).astype(BF16)

    a = normed(o_ref[...].astype(F32), gmla_ref, gm_ref)
    b = normed(y_ref[...], ghy_ref, gh_ref)
    z = (jnp.dot(a, w_ref[:MLA_W, :], preferred_element_type=F32)
         + jnp.dot(b, w_ref[MLA_W:, :], preferred_element_type=F32))
    r = lax.rsqrt(jnp.mean(z * z, axis=-1, keepdims=True) + EPS)
    out_ref[...] = x_ref[...] + gx_ref[0] * ((z * r) * pg_ref[...])


def _merge(o, px, yh, x2d, gx, lp, w_out, rows_per_mod, tm):
    m, d = x2d.shape
    per = rows_per_mod // tm
    return pl.pallas_call(
        _merge_kernel,
        grid=(m // tm,),
        in_specs=[pl.BlockSpec((tm, MLA_W), lambda i: (i, 0)),
                  pl.BlockSpec((tm, MLA_W), lambda i: (i, P_GM // MLA_W)),
                  pl.BlockSpec((tm, HY_W), lambda i: (i, 0)),
                  pl.BlockSpec((tm, HY_W), lambda i: (i, P_GH // HY_W)),
                  pl.BlockSpec((tm, d), lambda i: (i, 0)),
                  pl.BlockSpec((1, 1, d), lambda i: (i // per, 0, 0)),
                  pl.BlockSpec((1, MLA_W), lambda i: (0, 0)),
                  pl.BlockSpec((1, HY_W), lambda i: (0, 0)),
                  pl.BlockSpec((1, d), lambda i: (0, 0)),
                  _const_spec((MLA_W + HY_W, d))],
        out_specs=pl.BlockSpec((tm, d), lambda i: (i, 0)),
        out_shape=jax.ShapeDtypeStruct((m, d), F32),
        compiler_params=_cparams("arbitrary"),
        name="branch_merge_out_proj",
    )(o, px, yh, px, x2d, gx, lp["grp_g_mla"][None], lp["grp_g_hy"][None], lp["post_g"][None], w_out)


N_IN = Q_LORA + KV_LORA + D_ROPE + MLA_W + 4 * HY_W
OFF_KR = Q_LORA + KV_LORA
LANES = 128
WPREP_ROWS = 256


def _wprep_kernel(w_ref, o_ref):
    o_ref[0, :, 0:OFF_KR] = w_ref[0, :, 0:OFF_KR].astype(o_ref.dtype)
    t = w_ref[0, :, OFF_KR:OFF_KR + LANES]
    lane = lax.broadcasted_iota(jnp.int32, t.shape, 1)
    quarter = D_ROPE // 4
    rot = jnp.where((lane - D_ROPE) % (2 * quarter) < quarter,
                    -pltpu.roll(t, D_ROPE - quarter, 1), pltpu.roll(t, D_ROPE + quarter, 1))
    o_ref[0, :, OFF_KR:OFF_KR + LANES] = jnp.where(lane < D_ROPE, t, rot).astype(o_ref.dtype)
    o_ref[0, :, OFF_KR + LANES:P_GM] = jnp.zeros((t.shape[0], P_GM - OFF_KR - LANES), o_ref.dtype)
    src0 = OFF_KR + D_ROPE
    width = N_PROJ - P_GM
    blk = 1024
    for j in range(width // blk):
        lo = OFF_KR + j * blk
        hi = min(lo + blk + LANES, N_IN)
        x = w_ref[0, :, lo:hi]
        o_ref[0, :, P_GM + j * blk:P_GM + (j + 1) * blk] = x[:, src0 - OFF_KR:src0 - OFF_KR + blk].astype(o_ref.dtype)


def _prep_w_in(w_in):
    depth, d, n = w_in.shape
    assert n == N_IN
    return pl.pallas_call(
        _wprep_kernel,
        grid=(depth, d // WPREP_ROWS),
        in_specs=[pl.BlockSpec((1, WPREP_ROWS, n), lambda l, i: (l, i, 0))],
        out_specs=pl.BlockSpec((1, WPREP_ROWS, N_PROJ), lambda l, i: (l, i, 0)),
        out_shape=jax.ShapeDtypeStruct((depth, d, N_PROJ), BF16),
        compiler_params=_cparams("arbitrary", "arbitrary"),
        name="w_in_permute_cast",
    )(w_in)


def _prep_weights(lp):
    w_in_p = lp["w_in_p"]
    wq = lp["w_uq"].reshape(Q_LORA, N_HEADS, D_NOPE + D_ROPE)
    wq_r = wq[..., D_NOPE:]
    wq_cat = jnp.concatenate([wq[..., :D_NOPE], wq_r, _rot_half_cols(wq_r)], axis=-1)
    wq_cat = wq_cat.reshape(Q_LORA, N_HEADS * D_QK).astype(BF16)
    wkv = lp["w_ukv"].reshape(KV_LORA, N_HEADS, D_NOPE + D_V)
    wk = wkv[..., :D_NOPE].reshape(KV_LORA, N_HEADS * D_NOPE).astype(BF16)
    wvt = wkv[..., D_NOPE:].reshape(KV_LORA, MLA_W).T.astype(BF16)
    return w_in_p, wq_cat, wk, wvt, lp["w_out"].astype(BF16)


def _layer(x2d, c2d, mod, lp, bsz, L, Lc, update_ctx):
    d = D_MODEL
    w_in_p, wq_cat, wk, wvt, w_out = _prep_weights(lp)
    sh, sc, gt = mod[:, :d], mod[:, d:2 * d], mod[:, 2 * d:]
    sh_x, sc_x, g_x = (t[:bsz, None, :] for t in (sh, sc, gt))
    sh_c, sc_c, g_c = (t[bsz:bsz + 1, None, :] for t in (sh, sc, gt))
    pre_g = lp["pre_g"][None]

    px = _input_proj(x2d, sc_x, sh_x, pre_g, w_in_p, rows_per_mod=L, tm=1024)
    kv_blk = P_KV // (2 * KV_LORA)
    if update_ctx:
        pc = _input_proj(c2d, sc_c, sh_c, pre_g, w_in_p, rows_per_mod=bsz * Lc, tm=Lc)
        pc_kv, pc_blk = pc, kv_blk
    else:
        pc_kv = _input_proj(c2d, sc_c, sh_c, pre_g, w_in_p, rows_per_mod=bsz * Lc, tm=Lc,
                            col_lo=P_KV, col_hi=P_GM)
        pc_blk = 0

    tab_x = jnp.asarray(_rope_table(L))
    tab_c = jnp.asarray(_identity_rope_table(Lc))
    kv_g = lp["kv_norm_g"][None]
    kv_x = _kv_proj(px, kv_blk, kv_g, wk, wvt, tab_x, bsz, L, tm=512)
    kv_c = _kv_proj(pc_kv, pc_blk, kv_g, wk, wvt, tab_c, bsz, Lc, tm=Lc)
    q_x = _q_proj(px, lp["q_norm_g"][None], wq_cat, tab_x, rows_per_seq=L, tm=512)
    o_x = _attention(q_x.reshape(bsz, L, -1), [kv_x, kv_c], tq=2048, n_sub=4).reshape(bsz * L, MLA_W)

    kspec = _long_filter_spectrum(lp, L, ct=256)
    y_x = _long_hyena(px, lp, kspec, bsz, L, ct=256)
    x_new = _merge(o_x, px, y_x, x2d, g_x, lp, w_out, rows_per_mod=L, tm=512)

    c_new = c2d
    if update_ctx:
        q_c = _q_proj(pc, lp["q_norm_g"][None], wq_cat, tab_c, rows_per_seq=Lc, tm=Lc)
        o_c = _attention(q_c.reshape(bsz, Lc, -1), [kv_c], tq=Lc).reshape(bsz * Lc, MLA_W)
        kspec_c = _short_filter_spectrum(lp, Lc, ct=256)
        y_c = _short_hyena(pc, lp, kspec_c, bsz, Lc, ct=256)
        c_new = _merge(o_c, pc, y_c, c2d, g_c, lp, w_out, rows_per_mod=bsz * Lc, tm=Lc)
    return x_new, c_new


def kernel(x, c, ctx, c_ctx, ada_w, ada_b, pre_g, w_in, q_norm_g, w_uq, kv_norm_g, w_ukv, conv_w, conv_b,
           filt_w1, filt_b1, filt_freq, filt_w2, filt_b2, filt_w3, hy_D, grp_g_mla, grp_g_hy, w_out, post_g):
    bsz, L, d = x.shape
    Lc = ctx.shape[1]
    depth = ada_w.shape[0]
    assert d == D_MODEL and L == (FFT_N1 // 2) * FFT_N2 and bsz + 1 <= SUB
    cvec = jnp.concatenate([c, c_ctx[None], jnp.zeros((SUB - bsz - 1, d), F32)], axis=0)
    mod = _modulation(cvec, ada_w, ada_b)
    params = dict(pre_g=pre_g, w_in_p=_prep_w_in(w_in), q_norm_g=q_norm_g, w_uq=w_uq, kv_norm_g=kv_norm_g, w_ukv=w_ukv,
                  conv_w=conv_w, conv_b=conv_b, filt_w1=filt_w1, filt_b1=filt_b1, filt_freq=filt_freq,
                  filt_w2=filt_w2, filt_b2=filt_b2, filt_w3=filt_w3, hy_D=hy_D, grp_g_mla=grp_g_mla,
                  grp_g_hy=grp_g_hy, w_out=w_out, post_g=post_g)
    x2d = x.reshape(bsz * L, d)
    c2d = ctx.reshape(bsz * Lc, d)
    for l in range(depth):
        lp = {k: v[l] for k, v in params.items()}
        x2d, c2d = _layer(x2d, c2d, mod[l], lp, bsz, L, Lc, update_ctx=(l < depth - 1))
    return x2d.reshape(bsz, L, d)
```

```python
import functools
import math

import numpy as np
import jax
import jax.numpy as jnp
from jax import lax
from jax.experimental import pallas as pl
from jax.experimental.pallas import tpu as pltpu

F32 = jnp.float32
BF16 = jnp.bfloat16

D_MODEL = 2048
GRID_W = 64
N_HEADS = 8
D_NOPE = 128
D_ROPE = 64
D_V = 128
MLA_W = N_HEADS * D_V
Q_LORA = 512
KV_LORA = 256
ROPE_THETA = 10000.0
SCALE = (D_NOPE + D_ROPE) ** -0.5
Q_SCALE = SCALE * math.log2(math.e)
HY_W = 1024
FILT_EMB = 33
FILT_HIDDEN = 64
FILT_TARGET = 1e-2
FILT_FAST_DECAY = 0.3
FILT_SLOW_DECAY = 1.5
EPS = 1e-6

P_Q = 0
P_KV = 512
P_GM = 1024
P_HY = 2048
P_GH = 5120
N_PROJ = 6144
N_IN = Q_LORA + KV_LORA + D_ROPE + MLA_W + 4 * HY_W
BULK_ROW0 = Q_LORA + KV_LORA + D_ROPE
D_QK = 256

VMEM_LIMIT = 58 * 1024 * 1024

FFT_N1 = 64
FFT_N2 = 128
FFT_H1 = FFT_N1 // 2 + 1
SUB = 8
LANES = 128

SAFE_BOUND = 40.0
BOUND_SLACK = 1.03


def _cparams(*sem):
    return pltpu.CompilerParams(dimension_semantics=sem, vmem_limit_bytes=VMEM_LIMIT)


def _const_spec(shape):
    nd = len(shape)
    return pl.BlockSpec(shape, lambda *_: (0,) * nd, pipeline_mode=pl.Buffered(1))


@functools.lru_cache(maxsize=None)
def _dft_tables():
    n1 = np.arange(FFT_N1 // 2)
    k1 = np.arange(FFT_H1)
    eye = np.eye(SUB)
    ang = 2 * np.pi * np.outer(k1, n1) / FFT_N1
    kron_f = np.concatenate([np.kron(np.cos(ang), eye), np.kron(-np.sin(ang), eye)], 0)
    ck = np.where((k1 == 0) | (k1 == FFT_N1 // 2), 1.0, 2.0)
    angi = 2 * np.pi * np.outer(n1, k1) / FFT_N1
    kron_i = np.concatenate([np.kron(np.cos(angi) * ck, eye),
                             np.kron((-np.sin(angi) * ck)[:, 1:FFT_H1 - 1], eye)], 1) / (FFT_N1 * FFT_N2)
    n2 = np.arange(FFT_N2)
    kk = FFT_N1 * np.arange(FFT_N2)[None, :, None] + k1[:, None, None]
    a = 2 * np.pi * kk * n2[None, None, :] / (FFT_N1 * FFT_N2)
    mr, mi = np.cos(a), -np.sin(a)
    rb = np.concatenate([np.concatenate([mr, -mi], 2), np.concatenate([mi, mr], 2)], 1)
    rbt = np.transpose(rb, (0, 2, 1))
    return (kron_f.astype(np.float32), kron_i.astype(np.float32),
            rb.astype(np.float32), rbt.astype(np.float32))


@functools.lru_cache(maxsize=None)
def _small_dft_tables(L):
    nf = L + 1
    nfp = -(-nf // SUB) * SUB
    n = np.arange(L)
    k = np.arange(nf)
    ang = 2 * np.pi * np.outer(k, n) / (2 * L)
    fwd = np.zeros((2 * nfp, L))
    fwd[:nf] = np.cos(ang)
    fwd[nfp:nfp + nf] = -np.sin(ang)
    ck = np.where((k == 0) | (k == L), 1.0, 2.0)
    inv = np.zeros((L, 2 * nfp))
    inv[:, :nf] = np.cos(ang).T * ck / (2 * L)
    inv[:, nfp:nfp + nf] = -np.sin(ang).T * ck / (2 * L)
    return fwd.astype(np.float32), inv.astype(np.float32)


@functools.lru_cache(maxsize=None)
def _filter_features(L):
    t = np.linspace(0.0, 1.0, L)[:, None]
    bands = (FILT_EMB - 1) // 2
    f = np.linspace(1e-4, bands - 1, bands)[None, :]
    wpos = (2.0 * math.pi) * np.arange(L)[:, None] / L
    z = np.concatenate([t, np.cos(f * wpos), -np.sin(f * wpos)], axis=-1)
    return z.astype(np.float32)


@functools.lru_cache(maxsize=None)
def _filter_deltas():
    d = np.linspace(math.log(FILT_TARGET) / FILT_FAST_DECAY,
                    math.log(FILT_TARGET) / FILT_SLOW_DECAY, HY_W)
    return np.abs(d)[None, :].astype(np.float32)


@functools.lru_cache(maxsize=None)
def _rope_table(n):
    rows = n // GRID_W
    row = np.repeat(np.arange(rows, dtype=np.float64), GRID_W)
    col = np.tile(np.arange(GRID_W, dtype=np.float64), rows)
    nf = D_ROPE // 4
    inv = ROPE_THETA ** (-np.arange(nf, dtype=np.float64) / nf)
    ang = np.stack([row[:, None] * inv, col[:, None] * inv], axis=1)
    cos = np.broadcast_to(np.cos(ang)[:, :, None, :], (n, 2, 2, nf)).reshape(n, D_ROPE)
    sin = np.broadcast_to(np.sin(ang)[:, :, None, :], (n, 2, 2, nf)).reshape(n, D_ROPE)
    return np.concatenate([cos, sin], -1).astype(np.float32)


def _identity_rope_table(n):
    return np.concatenate([np.ones((n, D_ROPE), np.float32), np.zeros((n, D_ROPE), np.float32)], -1)


def _rot_half_cols(w):
    w4 = w.reshape(w.shape[:-1] + (2, 2, D_ROPE // 4))
    return jnp.stack([-w4[..., 1, :], w4[..., 0, :]], axis=-2).reshape(w.shape)


def _mod_kernel(c_ref, w_ref, b_ref, o_ref):
    c = c_ref[...]
    s = c * jax.nn.sigmoid(c)
    o_ref[0] = jnp.dot(s, w_ref[0], preferred_element_type=F32) + b_ref[0]


def _modulation(cvec, ada_w, ada_b):
    depth, d, n = ada_w.shape
    r = cvec.shape[0]
    tn = 512
    return pl.pallas_call(
        _mod_kernel,
        grid=(depth, n // tn),
        in_specs=[pl.BlockSpec((r, d), lambda l, j: (0, 0)),
                  pl.BlockSpec((1, d, tn), lambda l, j: (l, 0, j)),
                  pl.BlockSpec((1, 1, tn), lambda l, j: (l, 0, j))],
        out_specs=pl.BlockSpec((1, r, tn), lambda l, j: (l, 0, j)),
        out_shape=jax.ShapeDtypeStruct((depth, r, n), F32),
        compiler_params=_cparams("arbitrary", "arbitrary"),
        name="adaln_modulation",
    )(cvec, ada_w, ada_b.reshape(depth, 1, n))


_NT = (((1,), (1,)), ((), ()))


def _prenorm(x_ref, sc_ref, sh_ref, g_ref):
    x = x_ref[...]
    r = lax.rsqrt(jnp.mean(x * x, axis=-1, keepdims=True) + EPS)
    return ((x * r) * (g_ref[...] * (1.0 + sc_ref[0])) + sh_ref[0]).astype(BF16)


def _win_kernel(x_ref, sc_ref, sh_ref, g_ref, wh_ref, wt_ref, o_ref, hx_ref):
    j = pl.program_id(1)
    tn = o_ref.shape[1]

    @pl.when(j == 0)
    def _():
        hx_ref[...] = _prenorm(x_ref, sc_ref, sh_ref, g_ref)
        o_ref[...] = lax.dot_general(hx_ref[...], wh_ref[...], _NT, preferred_element_type=F32).astype(o_ref.dtype)

    @pl.when(j > 0)
    def _():
        row = pl.multiple_of(BULK_ROW0 + (j - 1) * tn, D_ROPE)
        o_ref[...] = lax.dot_general(hx_ref[...], wt_ref[0, pl.ds(row, tn), :], _NT,
                                     preferred_element_type=F32).astype(o_ref.dtype)


def _input_proj(x2d, sc, sh, pre_g, w_head, wt_all, layer, rows_per_mod, tm):
    m, d = x2d.shape
    tn = P_GM
    per = rows_per_mod // tm
    n_in = wt_all.shape[1]
    assert (N_PROJ - tn) == n_in - BULK_ROW0
    return pl.pallas_call(
        _win_kernel,
        grid=(m // tm, N_PROJ // tn),
        in_specs=[pl.BlockSpec((tm, d), lambda i, j: (i, 0)),
                  pl.BlockSpec((1, 1, d), lambda i, j: (i // per, 0, 0)),
                  pl.BlockSpec((1, 1, d), lambda i, j: (i // per, 0, 0)),
                  pl.BlockSpec((1, d), lambda i, j: (0, 0)),
                  _const_spec((tn, d)),
                  pl.BlockSpec((1, n_in, d), lambda i, j: (layer, 0, 0), pipeline_mode=pl.Buffered(1))],
        out_specs=pl.BlockSpec((tm, tn), lambda i, j: (i, j)),
        out_shape=jax.ShapeDtypeStruct((m, N_PROJ), BF16),
        scratch_shapes=[pltpu.VMEM((tm, d), BF16)],
        compiler_params=_cparams("arbitrary", "arbitrary"),
        name="prenorm_input_proj",
    )(x2d, sc, sh, pre_g, w_head, wt_all)


def _win_kv_kernel(x_ref, sc_ref, sh_ref, g_ref, wh_ref, o_ref):
    hx = _prenorm(x_ref, sc_ref, sh_ref, g_ref)
    o_ref[...] = lax.dot_general(hx, wh_ref[...], _NT, preferred_element_type=F32).astype(o_ref.dtype)


def _input_proj_kv(x2d, sc, sh, pre_g, w_head, tm):
    m, d = x2d.shape
    kw = 2 * KV_LORA
    return pl.pallas_call(
        _win_kv_kernel,
        grid=(m // tm,),
        in_specs=[pl.BlockSpec((tm, d), lambda i: (i, 0)),
                  pl.BlockSpec((1, 1, d), lambda i: (0, 0, 0)),
                  pl.BlockSpec((1, 1, d), lambda i: (0, 0, 0)),
                  pl.BlockSpec((1, d), lambda i: (0, 0)),
                  pl.BlockSpec((kw, d), lambda i: (P_KV // kw, 0))],
        out_specs=pl.BlockSpec((tm, kw), lambda i: (i, 0)),
        out_shape=jax.ShapeDtypeStruct((m, kw), BF16),
        compiler_params=_cparams("arbitrary"),
        name="prenorm_kv_input_proj",
    )(x2d, sc, sh, pre_g, w_head)


def _rope_lanes(v, tab):
    t = v * tab
    return t + pltpu.roll(t, D_ROPE, 1)


def _q_kernel(p_ref, g_ref, w_ref, tab_ref, o_ref):
    p = p_ref[...].astype(F32)
    r = lax.rsqrt(jnp.mean(p * p, axis=-1, keepdims=True) + EPS)
    cq = ((p * r) * g_ref[...]).astype(BF16)
    tab = tab_ref[...]
    for h in range(N_HEADS):
        lo = h * D_QK
        res = jnp.dot(cq, w_ref[:, lo:lo + D_QK], preferred_element_type=F32)
        o_ref[:, lo:lo + D_NOPE] = (res[:, :D_NOPE] * Q_SCALE).astype(o_ref.dtype)
        o_ref[:, lo + D_NOPE:lo + D_QK] = (_rope_lanes(res[:, D_NOPE:], tab) * Q_SCALE).astype(o_ref.dtype)


def _q_proj(px, q_g, wq, tab, rows_per_seq, tm):
    m = px.shape[0]
    per = rows_per_seq // tm
    return pl.pallas_call(
        _q_kernel,
        grid=(m // tm,),
        in_specs=[pl.BlockSpec((tm, Q_LORA), lambda i: (i, P_Q // Q_LORA)),
                  pl.BlockSpec((1, Q_LORA), lambda i: (0, 0)),
                  pl.BlockSpec((Q_LORA, N_HEADS * D_QK), lambda i: (0, 0)),
                  pl.BlockSpec((tm, 2 * D_ROPE), lambda i: (i % per, 0))],
        out_specs=pl.BlockSpec((tm, N_HEADS * D_QK), lambda i: (i, 0)),
        out_shape=jax.ShapeDtypeStruct((m, N_HEADS * D_QK), BF16),
        compiler_params=_cparams("arbitrary"),
        name="mla_q_proj",
    )(px, q_g, wq, tab)


def _kv_kernel(p_ref, g_ref, wk_ref, wvt_ref, tab_ref, k_ref, vt_ref, kn2_ref):
    p = p_ref[:, :KV_LORA].astype(F32)
    r = lax.rsqrt(jnp.mean(p * p, axis=-1, keepdims=True) + EPS)
    ckv = ((p * r) * g_ref[...]).astype(BF16)
    kr = _rope_lanes(p_ref[:, KV_LORA:KV_LORA + 2 * D_ROPE].astype(F32), tab_ref[...])
    lane = lax.broadcasted_iota(jnp.int32, kr.shape, 1)
    kr = jnp.where(lane < D_ROPE, kr, 0.0).astype(k_ref.dtype)
    kn = jnp.dot(ckv, wk_ref[...], preferred_element_type=F32).astype(k_ref.dtype)
    krf = kr.astype(F32)
    kr2 = jnp.sum(krf * krf, axis=-1, keepdims=True)
    norms = []
    for h in range(N_HEADS):
        kh = kn[:, h * D_NOPE:(h + 1) * D_NOPE]
        k_ref[0, :, h * D_QK:h * D_QK + D_NOPE] = kh
        k_ref[0, :, h * D_QK + D_NOPE:(h + 1) * D_QK] = kr
        khf = kh.astype(F32)
        n2 = jnp.max(jnp.sum(khf * khf, axis=-1, keepdims=True) + kr2, axis=0, keepdims=True)
        norms.append(jnp.broadcast_to(n2, (1, LANES)))
    vt = lax.dot_general(wvt_ref[...], ckv, (((1,), (1,)), ((), ())), preferred_element_type=F32)
    vt_ref[0] = vt.astype(vt_ref.dtype)

    norms = jnp.concatenate(norms, axis=0)

    @pl.when(pl.program_id(1) == 0)
    def _():
        kn2_ref[0] = norms

    @pl.when(pl.program_id(1) > 0)
    def _():
        kn2_ref[0] = jnp.maximum(kn2_ref[0], norms)


def _kv_proj(p, col_blk, kv_g, wk, wvt, tab, bsz, ltok, tm):
    per = ltok // tm
    kw = 2 * KV_LORA
    return pl.pallas_call(
        _kv_kernel,
        grid=(bsz, per),
        in_specs=[pl.BlockSpec((tm, kw), lambda b, t: (b * per + t, col_blk)),
                  pl.BlockSpec((1, KV_LORA), lambda b, t: (0, 0)),
                  pl.BlockSpec((KV_LORA, N_HEADS * D_NOPE), lambda b, t: (0, 0)),
                  pl.BlockSpec((MLA_W, KV_LORA), lambda b, t: (0, 0)),
                  pl.BlockSpec((tm, 2 * D_ROPE), lambda b, t: (t, 0))],
        out_specs=[pl.BlockSpec((1, tm, N_HEADS * D_QK), lambda b, t: (b, t, 0)),
                   pl.BlockSpec((1, MLA_W, tm), lambda b, t: (b, 0, t)),
                   pl.BlockSpec((1, N_HEADS, LANES), lambda b, t: (b, 0, 0))],
        out_shape=[jax.ShapeDtypeStruct((bsz, ltok, N_HEADS * D_QK), BF16),
                   jax.ShapeDtypeStruct((bsz, MLA_W, ltok), BF16),
                   jax.ShapeDtypeStruct((bsz, N_HEADS, LANES), F32)],
        compiler_params=_cparams("arbitrary", "arbitrary"),
        name="mla_kv_proj",
    )(p, kv_g, wk, wvt, tab)


def _attn_kernel(q_ref, *refs, n_sub, n_kv):
    k_refs, vt_refs, kn2_refs = refs[0:3 * n_kv:3], refs[1:3 * n_kv:3], refs[2:3 * n_kv:3]
    o_ref, s_ref = refs[3 * n_kv:]
    rows = q_ref.shape[1] // n_sub
    offs = [0]
    for k_ref in k_refs:
        offs.append(offs[-1] + k_ref.shape[1])

    h = pl.program_id(1)
    k2 = functools.reduce(jnp.maximum, [r[0, pl.ds(h, 1), :] for r in kn2_refs])
    k2 = jnp.max(k2, axis=1, keepdims=True)
    qf = q_ref[0].astype(F32)
    q2 = lax.dot_general(jnp.ones((SUB, D_QK), BF16), (qf * qf).astype(BF16), _NT,
                         preferred_element_type=F32)[0:1]
    bound = jnp.sqrt(q2 * k2) * BOUND_SLACK
    safe = jnp.max(bound) <= SAFE_BOUND

    def finish_out(t, ot, l):
        o_ref[0, t * rows:(t + 1) * rows, :] = (ot / l).T.astype(o_ref.dtype)

    @pl.when(safe)
    def _():
        for t in range(n_sub):
            q = q_ref[0, t * rows:(t + 1) * rows, :]
            b = bound[:, t * rows:(t + 1) * rows]
            l = jnp.zeros((1, rows), F32)
            ot = jnp.zeros((D_V, rows), F32)
            for k_ref, vt_ref in zip(k_refs, vt_refs):
                pt = jnp.exp2(lax.dot_general(k_ref[0], q, _NT, preferred_element_type=F32) - b)
                l = l + jnp.sum(pt, axis=0, keepdims=True)
                ot = ot + jnp.dot(vt_ref[0], pt.astype(BF16), preferred_element_type=F32)
            finish_out(t, ot, l)

    @pl.when(jnp.logical_not(safe))
    def _():
        def scores(t):
            q = q_ref[0, t * rows:(t + 1) * rows, :]
            for i, k_ref in enumerate(k_refs):
                s_ref[t % 2, offs[i]:offs[i + 1]] = lax.dot_general(k_ref[0], q, _NT, preferred_element_type=F32)

        def finish(t):
            st = s_ref[t % 2]
            m = jnp.max(st, axis=0, keepdims=True)
            pt = jnp.exp2(st - m)
            l = jnp.sum(pt, axis=0, keepdims=True)
            pt = pt.astype(BF16)
            ot = sum(jnp.dot(vt_ref[0], pt[offs[i]:offs[i + 1]], preferred_element_type=F32)
                     for i, vt_ref in enumerate(vt_refs))
            finish_out(t, ot, l)

        scores(0)
        for t in range(n_sub):
            if t + 1 < n_sub:
                scores(t + 1)
            finish(t)


def _attention(q, kv_sets, tq, n_sub=1):
    bsz, lq, _ = q.shape
    in_specs = [pl.BlockSpec((1, tq, D_QK), lambda b, h, i: (b, i, h))]
    args = [q]
    for k, vt, kn2 in kv_sets:
        lk = k.shape[1]
        in_specs += [pl.BlockSpec((1, lk, D_QK), lambda b, h, i: (b, 0, h)),
                     pl.BlockSpec((1, D_V, lk), lambda b, h, i: (b, h, 0)),
                     pl.BlockSpec((1, N_HEADS, LANES), lambda b, h, i: (b, 0, 0))]
        args += [k, vt, kn2]
    lk_all = sum(kv[0].shape[1] for kv in kv_sets)
    return pl.pallas_call(
        functools.partial(_attn_kernel, n_sub=n_sub, n_kv=len(kv_sets)),
        grid=(bsz, N_HEADS, lq // tq),
        in_specs=in_specs,
        out_specs=pl.BlockSpec((1, tq, D_V), lambda b, h, i: (b, i, h)),
        out_shape=jax.ShapeDtypeStruct((bsz, lq, MLA_W), BF16),
        scratch_shapes=[pltpu.VMEM((2, lk_all, tq // n_sub), F32)],
        compiler_params=_cparams("arbitrary", "arbitrary", "arbitrary"),
        name="mla_attention",
    )(*args)


def _filter_mlp(z_ref, w1_ref, b1_ref, fr_ref, w2_ref, b2_ref):
    hi = lax.Precision.HIGHEST
    fr = fr_ref[...]
    h = jnp.sin(fr * (jnp.dot(z_ref[...], w1_ref[...], precision=hi, preferred_element_type=F32) + b1_ref[...]))
    return jnp.sin(fr * (jnp.dot(h, w2_ref[...], precision=hi, preferred_element_type=F32) + b2_ref[...]))


def _filter_taps(h, z_ref, w3_ref, dl_ref, zero_first):
    taps = jnp.dot(h.astype(BF16), w3_ref[...].astype(BF16), preferred_element_type=F32)
    taps = taps * jnp.exp(-z_ref[:, 0:1] * dl_ref[...])
    if zero_first:
        row = lax.broadcasted_iota(jnp.int32, taps.shape, 0)
        taps = jnp.where(row == 0, 0.0, taps)
    return taps


def _fill_padded(pad_ref, src, n):
    ct = pad_ref.shape[1]
    pad_ref[0:SUB, :] = jnp.zeros((SUB, ct), F32)
    pad_ref[SUB + n:2 * SUB + n, :] = jnp.zeros((SUB, ct), F32)
    pad_ref[SUB:SUB + n, :] = src.astype(F32)


def _short_conv(pad_ref, w_ref, b_ref, r0, rows):
    n = rows + 2 * SUB
    win = pad_ref[r0:r0 + n, :]
    prev = pltpu.roll(win, 1, 0)[SUB:SUB + rows]
    nxt = pltpu.roll(win, n - 1, 0)[SUB:SUB + rows]
    return (b_ref[...] + prev * w_ref[0:1, :] + win[SUB:SUB + rows] * w_ref[1:2, :] + nxt * w_ref[2:3, :])


def _cmul(ur, ui, kr, ki):
    return ur * kr - ui * ki, ur * ki + ui * kr


CONV_CHUNK = 512
MID_UNROLL = 11


def _stage1(u_ref, kf_ref, a_ref):
    n1h, _, ct = u_ref.shape
    for j in range(FFT_N2 // SUB):
        slab = u_ref[:, j * SUB:(j + 1) * SUB, :].reshape(n1h * SUB, ct).astype(BF16)
        r = jnp.dot(kf_ref[...], slab, preferred_element_type=F32)
        a_ref[:, :, j * SUB:(j + 1) * SUB, :] = r.reshape(2, FFT_H1, SUB, ct)


def _stage2(a_ref, rb_ref, i):
    ct = a_ref.shape[-1]
    x = a_ref[:, i].reshape(2 * FFT_N2, ct).astype(BF16)
    return jnp.dot(rb_ref[i], x, preferred_element_type=F32)


def _lfilt_kernel(z_ref, w1_ref, b1_ref, fr_ref, w2_ref, b2_ref, w3f_ref, w3b_ref, dl_ref,
                  kf_ref, rb_ref, o_ref, h_ref, u_ref, a_ref, s_ref):
    ct = u_ref.shape[-1]

    @pl.when(pl.program_id(0) == 0)
    def _():
        h_ref[...] = _filter_mlp(z_ref, w1_ref, b1_ref, fr_ref, w2_ref, b2_ref)

    for w3_ref, back in ((w3f_ref, False), (w3b_ref, True)):
        taps = _filter_taps(h_ref[...], z_ref, w3_ref, dl_ref, zero_first=back)
        u_ref[...] = taps.reshape(u_ref.shape)
        _stage1(u_ref, kf_ref, a_ref)

        def body(i, carry, back=back):
            s = _stage2(a_ref, rb_ref, i).reshape(2, FFT_N2, ct)
            if back:
                o_ref[0, i] = (s_ref[0, i] + s[0]).astype(o_ref.dtype)
                o_ref[1, i] = (s_ref[1, i] - s[1]).astype(o_ref.dtype)
            else:
                s_ref[:, i] = s
            return carry

        lax.fori_loop(0, FFT_H1, body, 0, unroll=MID_UNROLL)


def _long_filter_spectrum(lp, L, ct):
    kron_f, _, rb, _ = _dft_tables()
    z = jnp.asarray(_filter_features(L))
    dl = jnp.asarray(_filter_deltas())
    nct = HY_W // ct
    full = lambda shape: pl.BlockSpec(shape, lambda j: (0,) * len(shape))
    return pl.pallas_call(
        _lfilt_kernel,
        grid=(nct,),
        in_specs=[full((L, FILT_EMB)), full((FILT_EMB, FILT_HIDDEN)), full((1, FILT_HIDDEN)),
                  full((1, FILT_HIDDEN)), full((FILT_HIDDEN, FILT_HIDDEN)), full((1, FILT_HIDDEN)),
                  pl.BlockSpec((FILT_HIDDEN, ct), lambda j: (0, j)),
                  pl.BlockSpec((FILT_HIDDEN, ct), lambda j: (0, j + nct)),
                  pl.BlockSpec((1, ct), lambda j: (0, j)),
                  _const_spec(kron_f.shape), _const_spec(rb.shape)],
        out_specs=pl.BlockSpec((2, FFT_H1, FFT_N2, ct), lambda j: (0, 0, 0, j)),
        out_shape=jax.ShapeDtypeStruct((2, FFT_H1, FFT_N2, HY_W), BF16),
        scratch_shapes=[pltpu.VMEM((L, FILT_HIDDEN), F32),
                        pltpu.VMEM((FFT_N1 // 2, FFT_N2, ct), F32),
                        pltpu.VMEM((2, FFT_H1, FFT_N2, ct), F32),
                        pltpu.VMEM((2, FFT_H1, FFT_N2, ct), F32)],
        compiler_params=_cparams("arbitrary"),
        name="hyena_filter_spectrum",
    )(z, lp["filt_w1"], lp["filt_b1"][None], lp["filt_freq"][None], lp["filt_w2"], lp["filt_b2"][None],
      lp["filt_w3"], lp["filt_w3"], dl, jnp.asarray(kron_f).astype(BF16), jnp.asarray(rb).astype(BF16))


def _lhyena_kernel(x0_ref, x1_ref, v_ref, w0_ref, w1_ref, wv_ref, b0_ref, b1_ref, bv_ref, d_ref,
                   ks_ref, kf_ref, ki_ref, rb_ref, rbt_ref, o_ref, pad_ref, u_ref, a_ref):
    n1h, n2, ct = u_ref.shape
    L = n1h * n2
    nchunk = L // CONV_CHUNK
    cpt = CONV_CHUNK // n2

    _fill_padded(pad_ref, x1_ref[0].reshape(L, ct), L)
    for c in range(nchunk):
        u_ref[c * cpt:(c + 1) * cpt] = _short_conv(pad_ref, w1_ref, b1_ref, c * CONV_CHUNK, CONV_CHUNK).reshape(cpt, n2, ct)
    _fill_padded(pad_ref, v_ref[0].reshape(L, ct), L)
    for c in range(nchunk):
        vv = _short_conv(pad_ref, wv_ref, bv_ref, c * CONV_CHUNK, CONV_CHUNK).reshape(cpt, n2, ct)
        u_ref[c * cpt:(c + 1) * cpt] = u_ref[c * cpt:(c + 1) * cpt] * vv

    _stage1(u_ref, kf_ref, a_ref)

    def body(i, carry):
        uh = _stage2(a_ref, rb_ref, i)
        kr = ks_ref[0, i].astype(F32)
        ki = ks_ref[1, i].astype(F32)
        zr, zi = _cmul(uh[:n2], uh[n2:], kr, ki)
        zz = jnp.concatenate([zr, zi], axis=0).astype(BF16)
        g = jnp.dot(rbt_ref[i], zz, preferred_element_type=F32)
        a_ref[:, i] = g.reshape(2, n2, ct)
        return carry

    lax.fori_loop(0, FFT_H1, body, 0, unroll=MID_UNROLL)

    for j in range(n2 // SUB):
        g = jnp.concatenate([
            a_ref[0, :, j * SUB:(j + 1) * SUB, :].reshape(FFT_H1 * SUB, ct),
            a_ref[1, 1:FFT_H1 - 1, j * SUB:(j + 1) * SUB, :].reshape((FFT_H1 - 2) * SUB, ct)], axis=0).astype(BF16)
        y = jnp.dot(ki_ref[...], g, preferred_element_type=F32).reshape(n1h, SUB, ct)
        o_ref[0, :, j * SUB:(j + 1) * SUB, :] = y + u_ref[:, j * SUB:(j + 1) * SUB, :] * d_ref[...]

    _fill_padded(pad_ref, x0_ref[0].reshape(L, ct), L)
    for c in range(nchunk):
        x0 = _short_conv(pad_ref, w0_ref, b0_ref, c * CONV_CHUNK, CONV_CHUNK).reshape(cpt, n2, ct)
        o_ref[0, c * cpt:(c + 1) * cpt] = o_ref[0, c * cpt:(c + 1) * cpt] * x0


def _long_hyena(px, lp, kspec, bsz, L, ct):
    kron_f, kron_i, rb, rbt = _dft_tables()
    n1h = L // FFT_N2
    px4 = px.reshape(bsz, n1h, FFT_N2, N_PROJ)
    nct = HY_W // ct
    c0 = P_HY // ct
    part = lambda k: pl.BlockSpec((1, n1h, FFT_N2, ct), lambda j, b: (b, 0, 0, c0 + k * nct + j))
    wpart = lambda k: pl.BlockSpec((3, ct), lambda j, b: (0, k * nct + j))
    bpart = lambda k: pl.BlockSpec((1, ct), lambda j, b: (0, k * nct + j))
    cb = lp["conv_b"][None]
    out = pl.pallas_call(
        _lhyena_kernel,
        grid=(nct, bsz),
        in_specs=[part(0), part(1), part(2), wpart(0), wpart(1), wpart(2), bpart(0), bpart(1), bpart(2),
                  pl.BlockSpec((1, ct), lambda j, b: (0, j)),
                  pl.BlockSpec((2, FFT_H1, FFT_N2, ct), lambda j, b: (0, 0, 0, j), pipeline_mode=pl.Buffered(1)),
                  _const_spec(kron_f.shape), _const_spec(kron_i.shape),
                  _const_spec(rb.shape), _const_spec(rbt.shape)],
        out_specs=pl.BlockSpec((1, n1h, FFT_N2, ct), lambda j, b: (b, 0, 0, j)),
        out_shape=jax.ShapeDtypeStruct((bsz, n1h, FFT_N2, HY_W), F32),
        scratch_shapes=[pltpu.VMEM((L + 2 * SUB, ct), F32),
                        pltpu.VMEM((n1h, FFT_N2, ct), F32),
                        pltpu.VMEM((2, FFT_H1, FFT_N2, ct), F32)],
        compiler_params=_cparams("arbitrary", "arbitrary"),
        name="hyena_long_conv",
    )(px4, px4, px4, lp["conv_w"], lp["conv_w"], lp["conv_w"], cb, cb, cb, lp["hy_D"][None], kspec,
      jnp.asarray(kron_f).astype(BF16), jnp.asarray(kron_i).astype(BF16),
      jnp.asarray(rb).astype(BF16), jnp.asarray(rbt).astype(BF16))
    return out.reshape(bsz * L, HY_W)


def _sfilt_kernel(z_ref, w1_ref, b1_ref, fr_ref, w2_ref, b2_ref, w3f_ref, w3b_ref, dl_ref, f_ref, o_ref):
    h = _filter_mlp(z_ref, w1_ref, b1_ref, fr_ref, w2_ref, b2_ref)
    nfp = f_ref.shape[0] // 2
    sf = jnp.dot(f_ref[...], _filter_taps(h, z_ref, w3f_ref, dl_ref, False).astype(BF16), preferred_element_type=F32)
    sb = jnp.dot(f_ref[...], _filter_taps(h, z_ref, w3b_ref, dl_ref, True).astype(BF16), preferred_element_type=F32)
    o_ref[0:nfp, :] = sf[:nfp] + sb[:nfp]
    o_ref[nfp:, :] = sf[nfp:] - sb[nfp:]


def _short_filter_spectrum(lp, L, ct):
    fwd, _ = _small_dft_tables(L)
    z = jnp.asarray(_filter_features(L))
    dl = jnp.asarray(_filter_deltas())
    nct = HY_W // ct
    full = lambda shape: pl.BlockSpec(shape, lambda j: (0,) * len(shape))
    return pl.pallas_call(
        _sfilt_kernel,
        grid=(nct,),
        in_specs=[full((L, FILT_EMB)), full((FILT_EMB, FILT_HIDDEN)), full((1, FILT_HIDDEN)),
                  full((1, FILT_HIDDEN)), full((FILT_HIDDEN, FILT_HIDDEN)), full((1, FILT_HIDDEN)),
                  pl.BlockSpec((FILT_HIDDEN, ct), lambda j: (0, j)),
                  pl.BlockSpec((FILT_HIDDEN, ct), lambda j: (0, j + nct)),
                  pl.BlockSpec((1, ct), lambda j: (0, j)),
                  full(fwd.shape)],
        out_specs=pl.BlockSpec((fwd.shape[0], ct), lambda j: (0, j)),
        out_shape=jax.ShapeDtypeStruct((fwd.shape[0], HY_W), F32),
        compiler_params=_cparams("arbitrary"),
        name="hyena_ctx_filter_spectrum",
    )(z, lp["filt_w1"], lp["filt_b1"][None], lp["filt_freq"][None], lp["filt_w2"], lp["filt_b2"][None],
      lp["filt_w3"], lp["filt_w3"], dl, jnp.asarray(fwd).astype(BF16))


def _shyena_kernel(x0_ref, x1_ref, v_ref, w0_ref, w1_ref, wv_ref, b0_ref, b1_ref, bv_ref, d_ref,
                   ks_ref, f_ref, fi_ref, o_ref, pad_ref):
    L = x0_ref.shape[0]
    nfp = f_ref.shape[0] // 2
    _fill_padded(pad_ref, x1_ref[...], L)
    u = _short_conv(pad_ref, w1_ref, b1_ref, 0, L)
    _fill_padded(pad_ref, v_ref[...], L)
    u = u * _short_conv(pad_ref, wv_ref, bv_ref, 0, L)
    uh = jnp.dot(f_ref[...], u.astype(BF16), preferred_element_type=F32)
    zr, zi = _cmul(uh[:nfp], uh[nfp:], ks_ref[0:nfp, :], ks_ref[nfp:, :])
    zz = jnp.concatenate([zr, zi], axis=0).astype(BF16)
    y = jnp.dot(fi_ref[...], zz, preferred_element_type=F32)
    _fill_padded(pad_ref, x0_ref[...], L)
    o_ref[...] = _short_conv(pad_ref, w0_ref, b0_ref, 0, L) * (y + u * d_ref[...])


def _short_hyena(pc, lp, kspec, bsz, L, ct):
    fwd, inv = _small_dft_tables(L)
    nct = HY_W // ct
    c0 = P_HY // ct
    part = lambda k: pl.BlockSpec((L, ct), lambda j, b: (b, c0 + k * nct + j))
    wpart = lambda k: pl.BlockSpec((3, ct), lambda j, b: (0, k * nct + j))
    bpart = lambda k: pl.BlockSpec((1, ct), lambda j, b: (0, k * nct + j))
    full = lambda shape: pl.BlockSpec(shape, lambda j, b: (0,) * len(shape))
    cb = lp["conv_b"][None]
    return pl.pallas_call(
        _shyena_kernel,
        grid=(nct, bsz),
        in_specs=[part(0), part(1), part(2), wpart(0), wpart(1), wpart(2), bpart(0), bpart(1), bpart(2),
                  pl.BlockSpec((1, ct), lambda j, b: (0, j)),
                  pl.BlockSpec((fwd.shape[0], ct), lambda j, b: (0, j)),
                  full(fwd.shape), full(inv.shape)],
        out_specs=pl.BlockSpec((L, ct), lambda j, b: (b, j)),
        out_shape=jax.ShapeDtypeStruct((bsz * L, HY_W), F32),
        scratch_shapes=[pltpu.VMEM((L + 2 * SUB, ct), F32)],
        compiler_params=_cparams("arbitrary", "arbitrary"),
        name="hyena_ctx_conv",
    )(pc, pc, pc, lp["conv_w"], lp["conv_w"], lp["conv_w"], cb, cb, cb, lp["hy_D"][None], kspec,
      jnp.asarray(fwd).astype(BF16), jnp.asarray(inv).astype(BF16))


def _merge_kernel(o_ref, gm_ref, y_ref, gh_ref, x_ref, gx_ref, gmla_ref, ghy_ref, pg_ref, w_ref, out_ref):
    def normed(t, g_ref, gate_ref):
        r = lax.rsqrt(jnp.mean(t * t, axis=-1, keepdims=True) + EPS)
        gate = gate_ref[...].astype(F32)
        return ((t * r) * g_ref[...] * (gate * jax.nn.sigmoid(gate))).astype(BF16)

    a = normed(o_ref[...].astype(F32), gmla_ref, gm_ref)
    b = normed(y_ref[...], ghy_ref, gh_ref)
    z = (jnp.dot(a, w_ref[:MLA_W, :], preferred_element_type=F32)
         + jnp.dot(b, w_ref[MLA_W:, :], preferred_element_type=F32))
    r = lax.rsqrt(jnp.mean(z * z, axis=-1, keepdims=True) + EPS)
    out_ref[...] = x_ref[...] + gx_ref[0] * ((z * r) * pg_ref[...])


def _merge(o, px, yh, x2d, gx, lp, w_out, rows_per_mod, tm):
    m, d = x2d.shape
    per = rows_per_mod // tm
    return pl.pallas_call(
        _merge_kernel,
        grid=(m // tm,),
        in_specs=[pl.BlockSpec((tm, MLA_W), lambda i: (i, 0)),
                  pl.BlockSpec((tm, MLA_W), lambda i: (i, P_GM // MLA_W)),
                  pl.BlockSpec((tm, HY_W), lambda i: (i, 0)),
                  pl.BlockSpec((tm, HY_W), lambda i: (i, P_GH // HY_W)),
                  pl.BlockSpec((tm, d), lambda i: (i, 0)),
                  pl.BlockSpec((1, 1, d), lambda i: (i // per, 0, 0)),
                  pl.BlockSpec((1, MLA_W), lambda i: (0, 0)),
                  pl.BlockSpec((1, HY_W), lambda i: (0, 0)),
                  pl.BlockSpec((1, d), lambda i: (0, 0)),
                  _const_spec((MLA_W + HY_W, d))],
        out_specs=pl.BlockSpec((tm, d), lambda i: (i, 0)),
        out_shape=jax.ShapeDtypeStruct((m, d), F32),
        compiler_params=_cparams("arbitrary"),
        name="branch_merge_out_proj",
    )(o, px, yh, px, x2d, gx, lp["grp_g_mla"][None], lp["grp_g_hy"][None], lp["post_g"][None], w_out)


def _head_rows(wt_all, layer):
    off_kr = Q_LORA + KV_LORA
    head = wt_all[layer, :BULK_ROW0]
    kr_rot = _rot_half_cols(head[off_kr:].T).T
    pad = jnp.zeros((P_GM - BULK_ROW0 - D_ROPE, head.shape[1]), head.dtype)
    return jnp.concatenate([head, kr_rot, pad], axis=0)


def _prep_weights(lp, wt_all, layer):
    w_head = _head_rows(wt_all, layer)
    wq = lp["w_uq"].reshape(Q_LORA, N_HEADS, D_NOPE + D_ROPE)
    wq_r = wq[..., D_NOPE:]
    wq_cat = jnp.concatenate([wq[..., :D_NOPE], wq_r, _rot_half_cols(wq_r)], axis=-1)
    wq_cat = wq_cat.reshape(Q_LORA, N_HEADS * D_QK).astype(BF16)
    wkv = lp["w_ukv"].reshape(KV_LORA, N_HEADS, D_NOPE + D_V)
    wk = wkv[..., :D_NOPE].reshape(KV_LORA, N_HEADS * D_NOPE).astype(BF16)
    wvt = wkv[..., D_NOPE:].reshape(KV_LORA, MLA_W).T.astype(BF16)
    return w_head, wq_cat, wk, wvt, lp["w_out"].astype(BF16)


def _layer(x2d, c2d, mod, lp, wt_all, layer, bsz, L, Lc, update_ctx):
    d = D_MODEL
    w_head, wq_cat, wk, wvt, w_out = _prep_weights(lp, wt_all, layer)
    sh, sc, gt = mod[:, :d], mod[:, d:2 * d], mod[:, 2 * d:]
    sh_x, sc_x, g_x = (t[:bsz, None, :] for t in (sh, sc, gt))
    sh_c, sc_c, g_c = (t[bsz:bsz + 1, None, :] for t in (sh, sc, gt))
    pre_g = lp["pre_g"][None]

    px = _input_proj(x2d, sc_x, sh_x, pre_g, w_head, wt_all, layer, rows_per_mod=L, tm=1024)
    kv_blk = P_KV // (2 * KV_LORA)
    if update_ctx:
        pc = _input_proj(c2d, sc_c, sh_c, pre_g, w_head, wt_all, layer, rows_per_mod=bsz * Lc, tm=Lc)
        pc_kv, pc_blk = pc, kv_blk
    else:
        pc_kv = _input_proj_kv(c2d, sc_c, sh_c, pre_g, w_head, tm=Lc)
        pc_blk = 0

    tab_x = jnp.asarray(_rope_table(L))
    tab_c = jnp.asarray(_identity_rope_table(Lc))
    kv_g = lp["kv_norm_g"][None]
    kv_x = _kv_proj(px, kv_blk, kv_g, wk, wvt, tab_x, bsz, L, tm=512)
    kv_c = _kv_proj(pc_kv, pc_blk, kv_g, wk, wvt, tab_c, bsz, Lc, tm=Lc)
    q_x = _q_proj(px, lp["q_norm_g"][None], wq_cat, tab_x, rows_per_seq=L, tm=512)
    o_x = _attention(q_x.reshape(bsz, L, -1), [kv_x, kv_c], tq=4096, n_sub=8).reshape(bsz * L, MLA_W)

    kspec = _long_filter_spectrum(lp, L, ct=256)
    y_x = _long_hyena(px, lp, kspec, bsz, L, ct=256)
    x_new = _merge(o_x, px, y_x, x2d, g_x, lp, w_out, rows_per_mod=L, tm=512)

    c_new = c2d
    if update_ctx:
        q_c = _q_proj(pc, lp["q_norm_g"][None], wq_cat, tab_c, rows_per_seq=Lc, tm=Lc)
        o_c = _attention(q_c.reshape(bsz, Lc, -1), [kv_c], tq=Lc).reshape(bsz * Lc, MLA_W)
        kspec_c = _short_filter_spectrum(lp, Lc, ct=256)
        y_c = _short_hyena(pc, lp, kspec_c, bsz, Lc, ct=256)
        c_new = _merge(o_c, pc, y_c, c2d, g_c, lp, w_out, rows_per_mod=bsz * Lc, tm=Lc)
    return x_new, c_new


def kernel(x, c, ctx, c_ctx, ada_w, ada_b, pre_g, w_in, q_norm_g, w_uq, kv_norm_g, w_ukv, conv_w, conv_b,
           filt_w1, filt_b1, filt_freq, filt_w2, filt_b2, filt_w3, hy_D, grp_g_mla, grp_g_hy, w_out, post_g):
    bsz, L, d = x.shape
    Lc = ctx.shape[1]
    depth = ada_w.shape[0]
    assert d == D_MODEL and L == (FFT_N1 // 2) * FFT_N2 and bsz + 1 <= SUB
    cvec = jnp.concatenate([c, c_ctx[None], jnp.zeros((SUB - bsz - 1, d), F32)], axis=0)
    mod = _modulation(cvec, ada_w, ada_b)
    wt_all = jnp.swapaxes(w_in, 1, 2).astype(BF16)
    params = dict(pre_g=pre_g, q_norm_g=q_norm_g, w_uq=w_uq, kv_norm_g=kv_norm_g, w_ukv=w_ukv,
                  conv_w=conv_w, conv_b=conv_b, filt_w1=filt_w1, filt_b1=filt_b1, filt_freq=filt_freq,
                  filt_w2=filt_w2, filt_b2=filt_b2, filt_w3=filt_w3, hy_D=hy_D, grp_g_mla=grp_g_mla,
                  grp_g_hy=grp_g_hy, w_out=w_out, post_g=post_g)
    x2d = x.reshape(bsz * L, d)
    c2d = ctx.reshape(bsz * Lc, d)
    for l in range(depth):
        lp = {k: v[l] for k, v in params.items()}
        x2d, c2d = _layer(x2d, c2d, mod[l], lp, wt_all, l, bsz, L, Lc, update_ctx=(l < depth - 1))
    return x2d.reshape(bsz, L, d)
```

```python
import functools
import math

import numpy as np
import jax
import jax.numpy as jnp
from jax import lax
from jax.experimental import pallas as pl
from jax.experimental.pallas import tpu as pltpu

F32 = jnp.float32
BF16 = jnp.bfloat16

D_MODEL = 2048
GRID_W = 64
N_HEADS = 8
D_NOPE = 128
D_ROPE = 64
D_V = 128
MLA_W = N_HEADS * D_V
Q_LORA = 512
KV_LORA = 256
ROPE_THETA = 10000.0
SCALE = (D_NOPE + D_ROPE) ** -0.5
Q_SCALE = SCALE * math.log2(math.e)
HY_W = 1024
FILT_EMB = 33
FILT_HIDDEN = 64
FILT_TARGET = 1e-2
FILT_FAST_DECAY = 0.3
FILT_SLOW_DECAY = 1.5
EPS = 1e-6

P_Q = 0
P_KV = 512
P_GM = 1024
P_HY = 2048
P_GH = 5120
N_PROJ = 6144
N_IN = Q_LORA + KV_LORA + D_ROPE + MLA_W + 4 * HY_W
BULK_ROW0 = Q_LORA + KV_LORA + D_ROPE
D_QK = 256

VMEM_LIMIT = 58 * 1024 * 1024

FFT_N1 = 64
FFT_N2 = 128
FFT_H1 = FFT_N1 // 2 + 1
SUB = 8
LANES = 128

SAFE_BOUND = 40.0
BOUND_SLACK = 1.03

TM_WIN = 1024
TM_QKV = 512
TQ_ATTN = 4096
ATTN_SUB = 8
FAST_ROWS = 2048
CT_HYENA = 256
TM_MERGE = 512


def _cparams(*sem):
    return pltpu.CompilerParams(dimension_semantics=sem, vmem_limit_bytes=VMEM_LIMIT)


def _const_spec(shape):
    nd = len(shape)
    return pl.BlockSpec(shape, lambda *_: (0,) * nd, pipeline_mode=pl.Buffered(1))


@functools.lru_cache(maxsize=None)
def _dft_tables():
    n1 = np.arange(FFT_N1 // 2)
    k1 = np.arange(FFT_H1)
    eye = np.eye(SUB)
    ang = 2 * np.pi * np.outer(k1, n1) / FFT_N1
    kron_f = np.concatenate([np.kron(np.cos(ang), eye), np.kron(-np.sin(ang), eye)], 0)
    ck = np.where((k1 == 0) | (k1 == FFT_N1 // 2), 1.0, 2.0)
    angi = 2 * np.pi * np.outer(n1, k1) / FFT_N1
    kron_i = np.concatenate([np.kron(np.cos(angi) * ck, eye),
                             np.kron((-np.sin(angi) * ck)[:, 1:FFT_H1 - 1], eye)], 1) / (FFT_N1 * FFT_N2)
    n2 = np.arange(FFT_N2)
    kk = FFT_N1 * np.arange(FFT_N2)[None, :, None] + k1[:, None, None]
    a = 2 * np.pi * kk * n2[None, None, :] / (FFT_N1 * FFT_N2)
    mr, mi = np.cos(a), -np.sin(a)
    rb = np.concatenate([np.concatenate([mr, -mi], 2), np.concatenate([mi, mr], 2)], 1)
    rbt = np.transpose(rb, (0, 2, 1))
    return (kron_f.astype(np.float32), kron_i.astype(np.float32),
            rb.astype(np.float32), rbt.astype(np.float32))


@functools.lru_cache(maxsize=None)
def _small_dft_tables(L):
    nf = L + 1
    nfp = -(-nf // SUB) * SUB
    n = np.arange(L)
    k = np.arange(nf)
    ang = 2 * np.pi * np.outer(k, n) / (2 * L)
    fwd = np.zeros((2 * nfp, L))
    fwd[:nf] = np.cos(ang)
    fwd[nfp:nfp + nf] = -np.sin(ang)
    ck = np.where((k == 0) | (k == L), 1.0, 2.0)
    inv = np.zeros((L, 2 * nfp))
    inv[:, :nf] = np.cos(ang).T * ck / (2 * L)
    inv[:, nfp:nfp + nf] = -np.sin(ang).T * ck / (2 * L)
    return fwd.astype(np.float32), inv.astype(np.float32)


@functools.lru_cache(maxsize=None)
def _filter_features(L):
    t = np.linspace(0.0, 1.0, L)[:, None]
    bands = (FILT_EMB - 1) // 2
    f = np.linspace(1e-4, bands - 1, bands)[None, :]
    wpos = (2.0 * math.pi) * np.arange(L)[:, None] / L
    z = np.concatenate([t, np.cos(f * wpos), -np.sin(f * wpos)], axis=-1)
    return z.astype(np.float32)


@functools.lru_cache(maxsize=None)
def _filter_deltas():
    d = np.linspace(math.log(FILT_TARGET) / FILT_FAST_DECAY,
                    math.log(FILT_TARGET) / FILT_SLOW_DECAY, HY_W)
    return np.abs(d)[None, :].astype(np.float32)


@functools.lru_cache(maxsize=None)
def _rope_table(n):
    rows = n // GRID_W
    row = np.repeat(np.arange(rows, dtype=np.float64), GRID_W)
    col = np.tile(np.arange(GRID_W, dtype=np.float64), rows)
    nf = D_ROPE // 4
    inv = ROPE_THETA ** (-np.arange(nf, dtype=np.float64) / nf)
    ang = np.stack([row[:, None] * inv, col[:, None] * inv], axis=1)
    cos = np.broadcast_to(np.cos(ang)[:, :, None, :], (n, 2, 2, nf)).reshape(n, D_ROPE)
    sin = np.broadcast_to(np.sin(ang)[:, :, None, :], (n, 2, 2, nf)).reshape(n, D_ROPE)
    return np.concatenate([cos, sin], -1).astype(np.float32)


def _identity_rope_table(n):
    return np.concatenate([np.ones((n, D_ROPE), np.float32), np.zeros((n, D_ROPE), np.float32)], -1)


def _rot_half_cols(w):
    w4 = w.reshape(w.shape[:-1] + (2, 2, D_ROPE // 4))
    return jnp.stack([-w4[..., 1, :], w4[..., 0, :]], axis=-2).reshape(w.shape)


def _mod_kernel(c_ref, w_ref, b_ref, o_ref):
    c = c_ref[...]
    s = c * jax.nn.sigmoid(c)
    o_ref[0] = jnp.dot(s, w_ref[0], preferred_element_type=F32) + b_ref[0]


def _modulation(cvec, ada_w, ada_b):
    depth, d, n = ada_w.shape
    r = cvec.shape[0]
    tn = 512
    return pl.pallas_call(
        _mod_kernel,
        grid=(depth, n // tn),
        in_specs=[pl.BlockSpec((r, d), lambda l, j: (0, 0)),
                  pl.BlockSpec((1, d, tn), lambda l, j: (l, 0, j)),
                  pl.BlockSpec((1, 1, tn), lambda l, j: (l, 0, j))],
        out_specs=pl.BlockSpec((1, r, tn), lambda l, j: (l, 0, j)),
        out_shape=jax.ShapeDtypeStruct((depth, r, n), F32),
        compiler_params=_cparams("arbitrary", "arbitrary"),
        name="adaln_modulation",
    )(cvec, ada_w, ada_b.reshape(depth, 1, n))


_NT = (((1,), (1,)), ((), ()))


def _prenorm(x_ref, sc_ref, sh_ref, g_ref):
    x = x_ref[...]
    r = lax.rsqrt(jnp.mean(x * x, axis=-1, keepdims=True) + EPS)
    return ((x * r) * (g_ref[...] * (1.0 + sc_ref[0])) + sh_ref[0]).astype(BF16)


def _win_kernel(x_ref, sc_ref, sh_ref, g_ref, wh_ref, wt_ref, o_ref, hx_ref):
    j = pl.program_id(1)
    tn = o_ref.shape[1]

    @pl.when(j == 0)
    def _():
        hx_ref[...] = _prenorm(x_ref, sc_ref, sh_ref, g_ref)
        o_ref[...] = lax.dot_general(hx_ref[...], wh_ref[...], _NT, preferred_element_type=F32).astype(o_ref.dtype)

    @pl.when(j > 0)
    def _():
        row = pl.multiple_of(BULK_ROW0 + (j - 1) * tn, D_ROPE)
        o_ref[...] = lax.dot_general(hx_ref[...], wt_ref[0, pl.ds(row, tn), :], _NT,
                                     preferred_element_type=F32).astype(o_ref.dtype)


def _input_proj(x2d, sc, sh, pre_g, w_head, wt_all, layer, rows_per_mod, tm):
    m, d = x2d.shape
    tn = P_GM
    per = rows_per_mod // tm
    n_in = wt_all.shape[1]
    assert (N_PROJ - tn) == n_in - BULK_ROW0
    return pl.pallas_call(
        _win_kernel,
        grid=(m // tm, N_PROJ // tn),
        in_specs=[pl.BlockSpec((tm, d), lambda i, j: (i, 0)),
                  pl.BlockSpec((1, 1, d), lambda i, j: (i // per, 0, 0)),
                  pl.BlockSpec((1, 1, d), lambda i, j: (i // per, 0, 0)),
                  pl.BlockSpec((1, d), lambda i, j: (0, 0)),
                  _const_spec((tn, d)),
                  pl.BlockSpec((1, n_in, d), lambda i, j: (layer, 0, 0), pipeline_mode=pl.Buffered(1))],
        out_specs=pl.BlockSpec((tm, tn), lambda i, j: (i, j)),
        out_shape=jax.ShapeDtypeStruct((m, N_PROJ), BF16),
        scratch_shapes=[pltpu.VMEM((tm, d), BF16)],
        compiler_params=_cparams("arbitrary", "arbitrary"),
        name="prenorm_input_proj",
    )(x2d, sc, sh, pre_g, w_head, wt_all)


def _win_kv_kernel(x_ref, sc_ref, sh_ref, g_ref, wh_ref, o_ref):
    hx = _prenorm(x_ref, sc_ref, sh_ref, g_ref)
    o_ref[...] = lax.dot_general(hx, wh_ref[...], _NT, preferred_element_type=F32).astype(o_ref.dtype)


def _input_proj_kv(x2d, sc, sh, pre_g, w_head, tm):
    m, d = x2d.shape
    kw = 2 * KV_LORA
    return pl.pallas_call(
        _win_kv_kernel,
        grid=(m // tm,),
        in_specs=[pl.BlockSpec((tm, d), lambda i: (i, 0)),
                  pl.BlockSpec((1, 1, d), lambda i: (0, 0, 0)),
                  pl.BlockSpec((1, 1, d), lambda i: (0, 0, 0)),
                  pl.BlockSpec((1, d), lambda i: (0, 0)),
                  pl.BlockSpec((kw, d), lambda i: (P_KV // kw, 0))],
        out_specs=pl.BlockSpec((tm, kw), lambda i: (i, 0)),
        out_shape=jax.ShapeDtypeStruct((m, kw), BF16),
        compiler_params=_cparams("arbitrary"),
        name="prenorm_kv_input_proj",
    )(x2d, sc, sh, pre_g, w_head)


def _rope_lanes(v, tab):
    t = v * tab
    return t + pltpu.roll(t, D_ROPE, 1)


def _q_kernel(p_ref, g_ref, w_ref, tab_ref, o_ref):
    p = p_ref[...].astype(F32)
    r = lax.rsqrt(jnp.mean(p * p, axis=-1, keepdims=True) + EPS)
    cq = ((p * r) * g_ref[...]).astype(BF16)
    tab = tab_ref[...]
    for h in range(N_HEADS):
        lo = h * D_QK
        res = jnp.dot(cq, w_ref[:, lo:lo + D_QK], preferred_element_type=F32)
        o_ref[:, lo:lo + D_NOPE] = (res[:, :D_NOPE] * Q_SCALE).astype(o_ref.dtype)
        o_ref[:, lo + D_NOPE:lo + D_QK] = (_rope_lanes(res[:, D_NOPE:], tab) * Q_SCALE).astype(o_ref.dtype)


def _q_proj(px, q_g, wq, tab, rows_per_seq, tm):
    m = px.shape[0]
    per = rows_per_seq // tm
    return pl.pallas_call(
        _q_kernel,
        grid=(m // tm,),
        in_specs=[pl.BlockSpec((tm, Q_LORA), lambda i: (i, P_Q // Q_LORA)),
                  pl.BlockSpec((1, Q_LORA), lambda i: (0, 0)),
                  pl.BlockSpec((Q_LORA, N_HEADS * D_QK), lambda i: (0, 0)),
                  pl.BlockSpec((tm, 2 * D_ROPE), lambda i: (i % per, 0))],
        out_specs=pl.BlockSpec((tm, N_HEADS * D_QK), lambda i: (i, 0)),
        out_shape=jax.ShapeDtypeStruct((m, N_HEADS * D_QK), BF16),
        compiler_params=_cparams("arbitrary"),
        name="mla_q_proj",
    )(px, q_g, wq, tab)


def _kv_kernel(p_ref, g_ref, wk_ref, wvt_ref, tab_ref, k_ref, vt_ref, kn2_ref):
    p = p_ref[:, :KV_LORA].astype(F32)
    r = lax.rsqrt(jnp.mean(p * p, axis=-1, keepdims=True) + EPS)
    ckv = ((p * r) * g_ref[...]).astype(BF16)
    kr = _rope_lanes(p_ref[:, KV_LORA:KV_LORA + 2 * D_ROPE].astype(F32), tab_ref[...])
    lane = lax.broadcasted_iota(jnp.int32, kr.shape, 1)
    kr = jnp.where(lane < D_ROPE, kr, 0.0).astype(k_ref.dtype)
    kn = jnp.dot(ckv, wk_ref[...], preferred_element_type=F32).astype(k_ref.dtype)
    krf = kr.astype(F32)
    kr2 = jnp.sum(krf * krf, axis=-1, keepdims=True)
    norms = []
    for h in range(N_HEADS):
        kh = kn[:, h * D_NOPE:(h + 1) * D_NOPE]
        k_ref[0, :, h * D_QK:h * D_QK + D_NOPE] = kh
        k_ref[0, :, h * D_QK + D_NOPE:(h + 1) * D_QK] = kr
        khf = kh.astype(F32)
        n2 = jnp.max(jnp.sum(khf * khf, axis=-1, keepdims=True) + kr2, axis=0, keepdims=True)
        norms.append(jnp.broadcast_to(n2, (1, LANES)))
    vt = lax.dot_general(wvt_ref[...], ckv, (((1,), (1,)), ((), ())), preferred_element_type=F32)
    vt_ref[0] = vt.astype(vt_ref.dtype)

    norms = jnp.concatenate(norms, axis=0)

    @pl.when(pl.program_id(1) == 0)
    def _():
        kn2_ref[0] = norms

    @pl.when(pl.program_id(1) > 0)
    def _():
        kn2_ref[0] = jnp.maximum(kn2_ref[0], norms)


def _kv_proj(p, col_blk, kv_g, wk, wvt, tab, bsz, ltok, tm):
    per = ltok // tm
    kw = 2 * KV_LORA
    return pl.pallas_call(
        _kv_kernel,
        grid=(bsz, per),
        in_specs=[pl.BlockSpec((tm, kw), lambda b, t: (b * per + t, col_blk)),
                  pl.BlockSpec((1, KV_LORA), lambda b, t: (0, 0)),
                  pl.BlockSpec((KV_LORA, N_HEADS * D_NOPE), lambda b, t: (0, 0)),
                  pl.BlockSpec((MLA_W, KV_LORA), lambda b, t: (0, 0)),
                  pl.BlockSpec((tm, 2 * D_ROPE), lambda b, t: (t, 0))],
        out_specs=[pl.BlockSpec((1, tm, N_HEADS * D_QK), lambda b, t: (b, t, 0)),
                   pl.BlockSpec((1, MLA_W, tm), lambda b, t: (b, 0, t)),
                   pl.BlockSpec((1, N_HEADS, LANES), lambda b, t: (b, 0, 0))],
        out_shape=[jax.ShapeDtypeStruct((bsz, ltok, N_HEADS * D_QK), BF16),
                   jax.ShapeDtypeStruct((bsz, MLA_W, ltok), BF16),
                   jax.ShapeDtypeStruct((bsz, N_HEADS, LANES), F32)],
        compiler_params=_cparams("arbitrary", "arbitrary"),
        name="mla_kv_proj",
    )(p, kv_g, wk, wvt, tab)


def _attn_kernel(q_ref, *refs, n_sub, n_kv):
    k_refs, vt_refs, kn2_refs = refs[0:3 * n_kv:3], refs[1:3 * n_kv:3], refs[2:3 * n_kv:3]
    o_ref, s_ref = refs[3 * n_kv:]
    rows = q_ref.shape[1] // n_sub
    offs = [0]
    for k_ref in k_refs:
        offs.append(offs[-1] + k_ref.shape[1])

    h = pl.program_id(1)
    k2 = functools.reduce(jnp.maximum, [r[0, pl.ds(h, 1), :] for r in kn2_refs])
    k2 = jnp.max(k2, axis=1, keepdims=True)
    qf = q_ref[0].astype(F32)
    q2 = lax.dot_general(jnp.ones((SUB, D_QK), BF16), (qf * qf).astype(BF16), _NT,
                         preferred_element_type=F32)[0:1]
    bound = jnp.sqrt(q2 * k2) * BOUND_SLACK
    safe = jnp.max(bound) <= SAFE_BOUND

    def finish_out(t, ot, l):
        o_ref[0, t * rows:(t + 1) * rows, :] = (ot / l).T.astype(o_ref.dtype)

    @pl.when(safe)
    def _():
        frows = min(FAST_ROWS, q_ref.shape[1])
        for r0 in range(0, q_ref.shape[1], frows):
            q = q_ref[0, r0:r0 + frows, :]
            b = bound[:, r0:r0 + frows]
            l = jnp.zeros((1, frows), F32)
            ot = jnp.zeros((D_V, frows), F32)
            for k_ref, vt_ref in zip(k_refs, vt_refs):
                pt = jnp.exp2(lax.dot_general(k_ref[0], q, _NT, preferred_element_type=F32) - b)
                l = l + jnp.sum(pt, axis=0, keepdims=True)
                ot = ot + jnp.dot(vt_ref[0], pt.astype(BF16), preferred_element_type=F32)
            o_ref[0, r0:r0 + frows, :] = (ot / l).T.astype(o_ref.dtype)

    @pl.when(jnp.logical_not(safe))
    def _():
        def scores(t):
            q = q_ref[0, t * rows:(t + 1) * rows, :]
            for i, k_ref in enumerate(k_refs):
                s_ref[t % 2, offs[i]:offs[i + 1]] = lax.dot_general(k_ref[0], q, _NT, preferred_element_type=F32)

        def finish(t):
            st = s_ref[t % 2]
            m = jnp.max(st, axis=0, keepdims=True)
            pt = jnp.exp2(st - m)
            l = jnp.sum(pt, axis=0, keepdims=True)
            pt = pt.astype(BF16)
            ot = sum(jnp.dot(vt_ref[0], pt[offs[i]:offs[i + 1]], preferred_element_type=F32)
                     for i, vt_ref in enumerate(vt_refs))
            finish_out(t, ot, l)

        scores(0)
        for t in range(n_sub):
            if t + 1 < n_sub:
                scores(t + 1)
            finish(t)


def _attention(q, kv_sets, tq, n_sub=1):
    bsz, lq, _ = q.shape
    in_specs = [pl.BlockSpec((1, tq, D_QK), lambda b, h, i: (b, i, h))]
    args = [q]
    for k, vt, kn2 in kv_sets:
        lk = k.shape[1]
        in_specs += [pl.BlockSpec((1, lk, D_QK), lambda b, h, i: (b, 0, h)),
                     pl.BlockSpec((1, D_V, lk), lambda b, h, i: (b, h, 0)),
                     pl.BlockSpec((1, N_HEADS, LANES), lambda b, h, i: (b, 0, 0))]
        args += [k, vt, kn2]
    lk_all = sum(kv[0].shape[1] for kv in kv_sets)
    return pl.pallas_call(
        functools.partial(_attn_kernel, n_sub=n_sub, n_kv=len(kv_sets)),
        grid=(bsz, N_HEADS, lq // tq),
        in_specs=in_specs,
        out_specs=pl.BlockSpec((1, tq, D_V), lambda b, h, i: (b, i, h)),
        out_shape=jax.ShapeDtypeStruct((bsz, lq, MLA_W), BF16),
        scratch_shapes=[pltpu.VMEM((2, lk_all, tq // n_sub), F32)],
        compiler_params=_cparams("arbitrary", "arbitrary", "arbitrary"),
        name="mla_attention",
    )(*args)


def _filter_mlp(z_ref, w1_ref, b1_ref, fr_ref, w2_ref, b2_ref):
    hi = lax.Precision.HIGHEST
    fr = fr_ref[...]
    h = jnp.sin(fr * (jnp.dot(z_ref[...], w1_ref[...], precision=hi, preferred_element_type=F32) + b1_ref[...]))
    return jnp.sin(fr * (jnp.dot(h, w2_ref[...], precision=hi, preferred_element_type=F32) + b2_ref[...]))


def _filter_taps(h, z_ref, w3_ref, dl_ref, zero_first):
    taps = jnp.dot(h.astype(BF16), w3_ref[...].astype(BF16), preferred_element_type=F32)
    taps = taps * jnp.exp(-z_ref[:, 0:1] * dl_ref[...])
    if zero_first:
        row = lax.broadcasted_iota(jnp.int32, taps.shape, 0)
        taps = jnp.where(row == 0, 0.0, taps)
    return taps


def _fill_padded(pad_ref, src, n):
    ct = pad_ref.shape[1]
    pad_ref[0:SUB, :] = jnp.zeros((SUB, ct), F32)
    pad_ref[SUB + n:2 * SUB + n, :] = jnp.zeros((SUB, ct), F32)
    pad_ref[SUB:SUB + n, :] = src.astype(F32)


def _short_conv(pad_ref, w_ref, b_ref, r0, rows):
    n = rows + 2 * SUB
    win = pad_ref[r0:r0 + n, :]
    prev = pltpu.roll(win, 1, 0)[SUB:SUB + rows]
    nxt = pltpu.roll(win, n - 1, 0)[SUB:SUB + rows]
    return (b_ref[...] + prev * w_ref[0:1, :] + win[SUB:SUB + rows] * w_ref[1:2, :] + nxt * w_ref[2:3, :])


def _cmul(ur, ui, kr, ki):
    return ur * kr - ui * ki, ur * ki + ui * kr


CONV_CHUNK = 512
MID_UNROLL = 11


def _stage1(u_ref, kf_ref, a_ref):
    n1h, _, ct = u_ref.shape
    for j in range(FFT_N2 // SUB):
        slab = u_ref[:, j * SUB:(j + 1) * SUB, :].reshape(n1h * SUB, ct).astype(BF16)
        r = jnp.dot(kf_ref[...], slab, preferred_element_type=F32)
        a_ref[:, :, j * SUB:(j + 1) * SUB, :] = r.reshape(2, FFT_H1, SUB, ct)


def _stage2(a_ref, rb_ref, i):
    ct = a_ref.shape[-1]
    x = a_ref[:, i].reshape(2 * FFT_N2, ct).astype(BF16)
    return jnp.dot(rb_ref[i], x, preferred_element_type=F32)


def _lfilt_kernel(z_ref, w1_ref, b1_ref, fr_ref, w2_ref, b2_ref, w3f_ref, w3b_ref, dl_ref,
                  kf_ref, rb_ref, o_ref, h_ref, u_ref, a_ref, s_ref):
    ct = u_ref.shape[-1]

    @pl.when(pl.program_id(0) == 0)
    def _():
        h_ref[...] = _filter_mlp(z_ref, w1_ref, b1_ref, fr_ref, w2_ref, b2_ref)

    for w3_ref, back in ((w3f_ref, False), (w3b_ref, True)):
        taps = _filter_taps(h_ref[...], z_ref, w3_ref, dl_ref, zero_first=back)
        u_ref[...] = taps.reshape(u_ref.shape)
        _stage1(u_ref, kf_ref, a_ref)

        def body(i, carry, back=back):
            s = _stage2(a_ref, rb_ref, i).reshape(2, FFT_N2, ct)
            if back:
                o_ref[0, i] = (s_ref[0, i] + s[0]).astype(o_ref.dtype)
                o_ref[1, i] = (s_ref[1, i] - s[1]).astype(o_ref.dtype)
            else:
                s_ref[:, i] = s
            return carry

        lax.fori_loop(0, FFT_H1, body, 0, unroll=MID_UNROLL)


def _long_filter_spectrum(lp, L, ct):
    kron_f, _, rb, _ = _dft_tables()
    z = jnp.asarray(_filter_features(L))
    dl = jnp.asarray(_filter_deltas())
    nct = HY_W // ct
    full = lambda shape: pl.BlockSpec(shape, lambda j: (0,) * len(shape))
    return pl.pallas_call(
        _lfilt_kernel,
        grid=(nct,),
        in_specs=[full((L, FILT_EMB)), full((FILT_EMB, FILT_HIDDEN)), full((1, FILT_HIDDEN)),
                  full((1, FILT_HIDDEN)), full((FILT_HIDDEN, FILT_HIDDEN)), full((1, FILT_HIDDEN)),
                  pl.BlockSpec((FILT_HIDDEN, ct), lambda j: (0, j)),
                  pl.BlockSpec((FILT_HIDDEN, ct), lambda j: (0, j + nct)),
                  pl.BlockSpec((1, ct), lambda j: (0, j)),
                  _const_spec(kron_f.shape), _const_spec(rb.shape)],
        out_specs=pl.BlockSpec((2, FFT_H1, FFT_N2, ct), lambda j: (0, 0, 0, j)),
        out_shape=jax.ShapeDtypeStruct((2, FFT_H1, FFT_N2, HY_W), BF16),
        scratch_shapes=[pltpu.VMEM((L, FILT_HIDDEN), F32),
                        pltpu.VMEM((FFT_N1 // 2, FFT_N2, ct), F32),
                        pltpu.VMEM((2, FFT_H1, FFT_N2, ct), F32),
                        pltpu.VMEM((2, FFT_H1, FFT_N2, ct), F32)],
        compiler_params=_cparams("arbitrary"),
        name="hyena_filter_spectrum",
    )(z, lp["filt_w1"], lp["filt_b1"][None], lp["filt_freq"][None], lp["filt_w2"], lp["filt_b2"][None],
      lp["filt_w3"], lp["filt_w3"], dl, jnp.asarray(kron_f).astype(BF16), jnp.asarray(rb).astype(BF16))


def _lhyena_kernel(x0_ref, x1_ref, v_ref, w0_ref, w1_ref, wv_ref, b0_ref, b1_ref, bv_ref, d_ref,
                   ks_ref, kf_ref, ki_ref, rb_ref, rbt_ref, o_ref, pad_ref, u_ref, a_ref):
    n1h, n2, ct = u_ref.shape
    L = n1h * n2
    nchunk = L // CONV_CHUNK
    cpt = CONV_CHUNK // n2

    _fill_padded(pad_ref, x1_ref[0].reshape(L, ct), L)
    for c in range(nchunk):
        u_ref[c * cpt:(c + 1) * cpt] = _short_conv(pad_ref, w1_ref, b1_ref, c * CONV_CHUNK, CONV_CHUNK).reshape(cpt, n2, ct)
    _fill_padded(pad_ref, v_ref[0].reshape(L, ct), L)
    for c in range(nchunk):
        vv = _short_conv(pad_ref, wv_ref, bv_ref, c * CONV_CHUNK, CONV_CHUNK).reshape(cpt, n2, ct)
        u_ref[c * cpt:(c + 1) * cpt] = u_ref[c * cpt:(c + 1) * cpt] * vv

    _stage1(u_ref, kf_ref, a_ref)

    def body(i, carry):
        uh = _stage2(a_ref, rb_ref, i)
        kr = ks_ref[0, i].astype(F32)
        ki = ks_ref[1, i].astype(F32)
        zr, zi = _cmul(uh[:n2], uh[n2:], kr, ki)
        zz = jnp.concatenate([zr, zi], axis=0).astype(BF16)
        g = jnp.dot(rbt_ref[i], zz, preferred_element_type=F32)
        a_ref[:, i] = g.reshape(2, n2, ct)
        return carry

    lax.fori_loop(0, FFT_H1, body, 0, unroll=MID_UNROLL)

    for j in range(n2 // SUB):
        g = jnp.concatenate([
            a_ref[0, :, j * SUB:(j + 1) * SUB, :].reshape(FFT_H1 * SUB, ct),
            a_ref[1, 1:FFT_H1 - 1, j * SUB:(j + 1) * SUB, :].reshape((FFT_H1 - 2) * SUB, ct)], axis=0).astype(BF16)
        y = jnp.dot(ki_ref[...], g, preferred_element_type=F32).reshape(n1h, SUB, ct)
        o_ref[0, :, j * SUB:(j + 1) * SUB, :] = y + u_ref[:, j * SUB:(j + 1) * SUB, :] * d_ref[...]

    _fill_padded(pad_ref, x0_ref[0].reshape(L, ct), L)
    for c in range(nchunk):
        x0 = _short_conv(pad_ref, w0_ref, b0_ref, c * CONV_CHUNK, CONV_CHUNK).reshape(cpt, n2, ct)
        o_ref[0, c * cpt:(c + 1) * cpt] = o_ref[0, c * cpt:(c + 1) * cpt] * x0


def _long_hyena(px, lp, kspec, bsz, L, ct):
    kron_f, kron_i, rb, rbt = _dft_tables()
    n1h = L // FFT_N2
    px4 = px.reshape(bsz, n1h, FFT_N2, N_PROJ)
    nct = HY_W // ct
    c0 = P_HY // ct
    part = lambda k: pl.BlockSpec((1, n1h, FFT_N2, ct), lambda j, b: (b, 0, 0, c0 + k * nct + j))
    wpart = lambda k: pl.BlockSpec((3, ct), lambda j, b: (0, k * nct + j))
    bpart = lambda k: pl.BlockSpec((1, ct), lambda j, b: (0, k * nct + j))
    cb = lp["conv_b"][None]
    out = pl.pallas_call(
        _lhyena_kernel,
        grid=(nct, bsz),
        in_specs=[part(0), part(1), part(2), wpart(0), wpart(1), wpart(2), bpart(0), bpart(1), bpart(2),
                  pl.BlockSpec((1, ct), lambda j, b: (0, j)),
                  pl.BlockSpec((2, FFT_H1, FFT_N2, ct), lambda j, b: (0, 0, 0, j), pipeline_mode=pl.Buffered(1)),
                  _const_spec(kron_f.shape), _const_spec(kron_i.shape),
                  _const_spec(rb.shape), _const_spec(rbt.shape)],
        out_specs=pl.BlockSpec((1, n1h, FFT_N2, ct), lambda j, b: (b, 0, 0, j)),
        out_shape=jax.ShapeDtypeStruct((bsz, n1h, FFT_N2, HY_W), F32),
        scratch_shapes=[pltpu.VMEM((L + 2 * SUB, ct), F32),
                        pltpu.VMEM((n1h, FFT_N2, ct), F32),
                        pltpu.VMEM((2, FFT_H1, FFT_N2, ct), F32)],
        compiler_params=_cparams("arbitrary", "arbitrary"),
        name="hyena_long_conv",
    )(px4, px4, px4, lp["conv_w"], lp["conv_w"], lp["conv_w"], cb, cb, cb, lp["hy_D"][None], kspec,
      jnp.asarray(kron_f).astype(BF16), jnp.asarray(kron_i).astype(BF16),
      jnp.asarray(rb).astype(BF16), jnp.asarray(rbt).astype(BF16))
    return out.reshape(bsz * L, HY_W)


def _sfilt_kernel(z_ref, w1_ref, b1_ref, fr_ref, w2_ref, b2_ref, w3f_ref, w3b_ref, dl_ref, f_ref, o_ref):
    h = _filter_mlp(z_ref, w1_ref, b1_ref, fr_ref, w2_ref, b2_ref)
    nfp = f_ref.shape[0] // 2
    sf = jnp.dot(f_ref[...], _filter_taps(h, z_ref, w3f_ref, dl_ref, False).astype(BF16), preferred_element_type=F32)
    sb = jnp.dot(f_ref[...], _filter_taps(h, z_ref, w3b_ref, dl_ref, True).astype(BF16), preferred_element_type=F32)
    o_ref[0:nfp, :] = sf[:nfp] + sb[:nfp]
    o_ref[nfp:, :] = sf[nfp:] - sb[nfp:]


def _short_filter_spectrum(lp, L, ct):
    fwd, _ = _small_dft_tables(L)
    z = jnp.asarray(_filter_features(L))
    dl = jnp.asarray(_filter_deltas())
    nct = HY_W // ct
    full = lambda shape: pl.BlockSpec(shape, lambda j: (0,) * len(shape))
    return pl.pallas_call(
        _sfilt_kernel,
        grid=(nct,),
        in_specs=[full((L, FILT_EMB)), full((FILT_EMB, FILT_HIDDEN)), full((1, FILT_HIDDEN)),
                  full((1, FILT_HIDDEN)), full((FILT_HIDDEN, FILT_HIDDEN)), full((1, FILT_HIDDEN)),
                  pl.BlockSpec((FILT_HIDDEN, ct), lambda j: (0, j)),
                  pl.BlockSpec((FILT_HIDDEN, ct), lambda j: (0, j + nct)),
                  pl.BlockSpec((1, ct), lambda j: (0, j)),
                  full(fwd.shape)],
        out_specs=pl.BlockSpec((fwd.shape[0], ct), lambda j: (0, j)),
        out_shape=jax.ShapeDtypeStruct((fwd.shape[0], HY_W), F32),
        compiler_params=_cparams("arbitrary"),
        name="hyena_ctx_filter_spectrum",
    )(z, lp["filt_w1"], lp["filt_b1"][None], lp["filt_freq"][None], lp["filt_w2"], lp["filt_b2"][None],
      lp["filt_w3"], lp["filt_w3"], dl, jnp.asarray(fwd).astype(BF16))


def _shyena_kernel(x0_ref, x1_ref, v_ref, w0_ref, w1_ref, wv_ref, b0_ref, b1_ref, bv_ref, d_ref,
                   ks_ref, f_ref, fi_ref, o_ref, pad_ref):
    L = x0_ref.shape[0]
    nfp = f_ref.shape[0] // 2
    _fill_padded(pad_ref, x1_ref[...], L)
    u = _short_conv(pad_ref, w1_ref, b1_ref, 0, L)
    _fill_padded(pad_ref, v_ref[...], L)
    u = u * _short_conv(pad_ref, wv_ref, bv_ref, 0, L)
    uh = jnp.dot(f_ref[...], u.astype(BF16), preferred_element_type=F32)
    zr, zi = _cmul(uh[:nfp], uh[nfp:], ks_ref[0:nfp, :], ks_ref[nfp:, :])
    zz = jnp.concatenate([zr, zi], axis=0).astype(BF16)
    y = jnp.dot(fi_ref[...], zz, preferred_element_type=F32)
    _fill_padded(pad_ref, x0_ref[...], L)
    o_ref[...] = _short_conv(pad_ref, w0_ref, b0_ref, 0, L) * (y + u * d_ref[...])


def _short_hyena(pc, lp, kspec, bsz, L, ct):
    fwd, inv = _small_dft_tables(L)
    nct = HY_W // ct
    c0 = P_HY // ct
    part = lambda k: pl.BlockSpec((L, ct), lambda j, b: (b, c0 + k * nct + j))
    wpart = lambda k: pl.BlockSpec((3, ct), lambda j, b: (0, k * nct + j))
    bpart = lambda k: pl.BlockSpec((1, ct), lambda j, b: (0, k * nct + j))
    full = lambda shape: pl.BlockSpec(shape, lambda j, b: (0,) * len(shape))
    cb = lp["conv_b"][None]
    return pl.pallas_call(
        _shyena_kernel,
        grid=(nct, bsz),
        in_specs=[part(0), part(1), part(2), wpart(0), wpart(1), wpart(2), bpart(0), bpart(1), bpart(2),
                  pl.BlockSpec((1, ct), lambda j, b: (0, j)),
                  pl.BlockSpec((fwd.shape[0], ct), lambda j, b: (0, j)),
                  full(fwd.shape), full(inv.shape)],
        out_specs=pl.BlockSpec((L, ct), lambda j, b: (b, j)),
        out_shape=jax.ShapeDtypeStruct((bsz * L, HY_W), F32),
        scratch_shapes=[pltpu.VMEM((L + 2 * SUB, ct), F32)],
        compiler_params=_cparams("arbitrary", "arbitrary"),
        name="hyena_ctx_conv",
    )(pc, pc, pc, lp["conv_w"], lp["conv_w"], lp["conv_w"], cb, cb, cb, lp["hy_D"][None], kspec,
      jnp.asarray(fwd).astype(BF16), jnp.asarray(inv).astype(BF16))


def _merge_kernel(o_ref, gm_ref, y_ref, gh_ref, x_ref, gx_ref, gmla_ref, ghy_ref, pg_ref, w_ref, out_ref):
    def normed(t, g_ref, gate_ref):
        r = lax.rsqrt(jnp.mean(t * t, axis=-1, keepdims=True) + EPS)
        gate = gate_ref[...].astype(F32)
        return ((t * r) * g_ref[...] * (gate * jax.nn.sigmoid(gate))).astype(BF16)

    a = normed(o_ref[...].astype(F32), gmla_ref, gm_ref)
    b = normed(y_ref[...], ghy_ref, gh_ref)
    z = (jnp.dot(a, w_ref[:MLA_W, :], preferred_element_type=F32)
         + jnp.dot(b, w_ref[MLA_W:, :], preferred_element_type=F32))
    r = lax.rsqrt(jnp.mean(z * z, axis=-1, keepdims=True) + EPS)
    out_ref[...] = x_ref[...] + gx_ref[0] * ((z * r) * pg_ref[...])


def _merge(o, px, yh, x2d, gx, lp, w_out, rows_per_mod, tm):
    m, d = x2d.shape
    per = rows_per_mod // tm
    return pl.pallas_call(
        _merge_kernel,
        grid=(m // tm,),
        in_specs=[pl.BlockSpec((tm, MLA_W), lambda i: (i, 0)),
                  pl.BlockSpec((tm, MLA_W), lambda i: (i, P_GM // MLA_W)),
                  pl.BlockSpec((tm, HY_W), lambda i: (i, 0)),
                  pl.BlockSpec((tm, HY_W), lambda i: (i, P_GH // HY_W)),
                  pl.BlockSpec((tm, d), lambda i: (i, 0)),
                  pl.BlockSpec((1, 1, d), lambda i: (i // per, 0, 0)),
                  pl.BlockSpec((1, MLA_W), lambda i: (0, 0)),
                  pl.BlockSpec((1, HY_W), lambda i: (0, 0)),
                  pl.BlockSpec((1, d), lambda i: (0, 0)),
                  _const_spec((MLA_W + HY_W, d))],
        out_specs=pl.BlockSpec((tm, d), lambda i: (i, 0)),
        out_shape=jax.ShapeDtypeStruct((m, d), F32),
        compiler_params=_cparams("arbitrary"),
        name="branch_merge_out_proj",
    )(o, px, yh, px, x2d, gx, lp["grp_g_mla"][None], lp["grp_g_hy"][None], lp["post_g"][None], w_out)


def _head_rows(wt_all, layer):
    off_kr = Q_LORA + KV_LORA
    head = wt_all[layer, :BULK_ROW0]
    kr_rot = _rot_half_cols(head[off_kr:].T).T
    pad = jnp.zeros((P_GM - BULK_ROW0 - D_ROPE, head.shape[1]), head.dtype)
    return jnp.concatenate([head, kr_rot, pad], axis=0)


def _prep_weights(lp, wt_all, layer):
    w_head = _head_rows(wt_all, layer)
    wq = lp["w_uq"].reshape(Q_LORA, N_HEADS, D_NOPE + D_ROPE)
    wq_r = wq[..., D_NOPE:]
    wq_cat = jnp.concatenate([wq[..., :D_NOPE], wq_r, _rot_half_cols(wq_r)], axis=-1)
    wq_cat = wq_cat.reshape(Q_LORA, N_HEADS * D_QK).astype(BF16)
    wkv = lp["w_ukv"].reshape(KV_LORA, N_HEADS, D_NOPE + D_V)
    wk = wkv[..., :D_NOPE].reshape(KV_LORA, N_HEADS * D_NOPE).astype(BF16)
    wvt = wkv[..., D_NOPE:].reshape(KV_LORA, MLA_W).T.astype(BF16)
    return w_head, wq_cat, wk, wvt, lp["w_out"].astype(BF16)


def _layer(x2d, c2d, mod, lp, wt_all, layer, bsz, L, Lc, update_ctx):
    d = D_MODEL
    w_head, wq_cat, wk, wvt, w_out = _prep_weights(lp, wt_all, layer)
    sh, sc, gt = mod[:, :d], mod[:, d:2 * d], mod[:, 2 * d:]
    sh_x, sc_x, g_x = (t[:bsz, None, :] for t in (sh, sc, gt))
    sh_c, sc_c, g_c = (t[bsz:bsz + 1, None, :] for t in (sh, sc, gt))
    pre_g = lp["pre_g"][None]

    px = _input_proj(x2d, sc_x, sh_x, pre_g, w_head, wt_all, layer, rows_per_mod=L, tm=TM_WIN)
    kv_blk = P_KV // (2 * KV_LORA)
    if update_ctx:
        pc = _input_proj(c2d, sc_c, sh_c, pre_g, w_head, wt_all, layer, rows_per_mod=bsz * Lc, tm=Lc)
        pc_kv, pc_blk = pc, kv_blk
    else:
        pc_kv = _input_proj_kv(c2d, sc_c, sh_c, pre_g, w_head, tm=Lc)
        pc_blk = 0

    tab_x = jnp.asarray(_rope_table(L))
    tab_c = jnp.asarray(_identity_rope_table(Lc))
    kv_g = lp["kv_norm_g"][None]
    kv_x = _kv_proj(px, kv_blk, kv_g, wk, wvt, tab_x, bsz, L, tm=TM_QKV)
    kv_c = _kv_proj(pc_kv, pc_blk, kv_g, wk, wvt, tab_c, bsz, Lc, tm=Lc)
    q_x = _q_proj(px, lp["q_norm_g"][None], wq_cat, tab_x, rows_per_seq=L, tm=TM_QKV)
    o_x = _attention(q_x.reshape(bsz, L, -1), [kv_x, kv_c], tq=TQ_ATTN, n_sub=ATTN_SUB).reshape(bsz * L, MLA_W)

    kspec = _long_filter_spectrum(lp, L, ct=CT_HYENA)
    y_x = _long_hyena(px, lp, kspec, bsz, L, ct=CT_HYENA)
    x_new = _merge(o_x, px, y_x, x2d, g_x, lp, w_out, rows_per_mod=L, tm=TM_MERGE)

    c_new = c2d
    if update_ctx:
        q_c = _q_proj(pc, lp["q_norm_g"][None], wq_cat, tab_c, rows_per_seq=Lc, tm=Lc)
        o_c = _attention(q_c.reshape(bsz, Lc, -1), [kv_c], tq=Lc).reshape(bsz * Lc, MLA_W)
        kspec_c = _short_filter_spectrum(lp, Lc, ct=CT_HYENA)
        y_c = _short_hyena(pc, lp, kspec_c, bsz, Lc, ct=CT_HYENA)
        c_new = _merge(o_c, pc, y_c, c2d, g_c, lp, w_out, rows_per_mod=bsz * Lc, tm=Lc)
    return x_new, c_new


def kernel(x, c, ctx, c_ctx, ada_w, ada_b, pre_g, w_in, q_norm_g, w_uq, kv_norm_g, w_ukv, conv_w, conv_b,
           filt_w1, filt_b1, filt_freq, filt_w2, filt_b2, filt_w3, hy_D, grp_g_mla, grp_g_hy, w_out, post_g):
    bsz, L, d = x.shape
    Lc = ctx.shape[1]
    depth = ada_w.shape[0]
    assert d == D_MODEL and L == (FFT_N1 // 2) * FFT_N2 and bsz + 1 <= SUB
    cvec = jnp.concatenate([c, c_ctx[None], jnp.zeros((SUB - bsz - 1, d), F32)], axis=0)
    mod = _modulation(cvec, ada_w, ada_b)
    wt_all = jnp.swapaxes(w_in, 1, 2).astype(BF16)
    params = dict(pre_g=pre_g, q_norm_g=q_norm_g, w_uq=w_uq, kv_norm_g=kv_norm_g, w_ukv=w_ukv,
                  conv_w=conv_w, conv_b=conv_b, filt_w1=filt_w1, filt_b1=filt_b1, filt_freq=filt_freq,
                  filt_w2=filt_w2, filt_b2=filt_b2, filt_w3=filt_w3, hy_D=hy_D, grp_g_mla=grp_g_mla,
                  grp_g_hy=grp_g_hy, w_out=w_out, post_g=post_g)
    x2d = x.reshape(bsz * L, d)
    c2d = ctx.reshape(bsz * Lc, d)
    for l in range(depth):
        lp = {k: v[l] for k, v in params.items()}
        x2d, c2d = _layer(x2d, c2d, mod[l], lp, wt_all, l, bsz, L, Lc, update_ctx=(l < depth - 1))
    return x2d.reshape(bsz, L, d)
```

```python
import functools
import math

import numpy as np
import jax
import jax.numpy as jnp
from jax import lax
from jax.experimental import pallas as pl
from jax.experimental.pallas import tpu as pltpu

F32 = jnp.float32
BF16 = jnp.bfloat16

D_MODEL = 2048
GRID_W = 64
N_HEADS = 8
D_NOPE = 128
D_ROPE = 64
D_V = 128
MLA_W = N_HEADS * D_V
Q_LORA = 512
KV_LORA = 256
ROPE_THETA = 10000.0
SCALE = (D_NOPE + D_ROPE) ** -0.5
Q_SCALE = SCALE * math.log2(math.e)
HY_W = 1024
FILT_EMB = 33
FILT_HIDDEN = 64
FILT_TARGET = 1e-2
FILT_FAST_DECAY = 0.3
FILT_SLOW_DECAY = 1.5
EPS = 1e-6

P_Q = 0
P_KV = 512
P_GM = 1024
P_HY = 2048
P_GH = 5120
N_PROJ = 6144
N_IN = Q_LORA + KV_LORA + D_ROPE + MLA_W + 4 * HY_W
BULK_ROW0 = Q_LORA + KV_LORA + D_ROPE
D_QK = 256

VMEM_LIMIT = 58 * 1024 * 1024

FFT_N1 = 64
FFT_N2 = 128
FFT_H1 = FFT_N1 // 2 + 1
SUB = 8
LANES = 128

SAFE_BOUND = 40.0
BOUND_SLACK = 1.03

TM_WIN = 1024
TM_QKV = 512
TQ_ATTN = 4096
ATTN_SUB = 8
FAST_ROWS = 2048
CT_HYENA = 256
TM_MERGE = 512


def _cparams(*sem):
    return pltpu.CompilerParams(dimension_semantics=sem, vmem_limit_bytes=VMEM_LIMIT)


def _const_spec(shape):
    nd = len(shape)
    return pl.BlockSpec(shape, lambda *_: (0,) * nd, pipeline_mode=pl.Buffered(1))


@functools.lru_cache(maxsize=None)
def _dft_tables():
    n1 = np.arange(FFT_N1 // 2)
    k1 = np.arange(FFT_H1)
    eye = np.eye(SUB)
    ang = 2 * np.pi * np.outer(k1, n1) / FFT_N1
    kron_f = np.concatenate([np.kron(np.cos(ang), eye), np.kron(-np.sin(ang), eye)], 0)
    ck = np.where((k1 == 0) | (k1 == FFT_N1 // 2), 1.0, 2.0)
    angi = 2 * np.pi * np.outer(n1, k1) / FFT_N1
    kron_i = np.concatenate([np.kron(np.cos(angi) * ck, eye),
                             np.kron((-np.sin(angi) * ck)[:, 1:FFT_H1 - 1], eye)], 1) / (FFT_N1 * FFT_N2)
    n2 = np.arange(FFT_N2)
    kk = FFT_N1 * np.arange(FFT_N2)[None, :, None] + k1[:, None, None]
    a = 2 * np.pi * kk * n2[None, None, :] / (FFT_N1 * FFT_N2)
    mr, mi = np.cos(a), -np.sin(a)
    rb = np.concatenate([np.concatenate([mr, -mi], 2), np.concatenate([mi, mr], 2)], 1)
    rbt = np.transpose(rb, (0, 2, 1))
    return (kron_f.astype(np.float32), kron_i.astype(np.float32),
            rb.astype(np.float32), rbt.astype(np.float32))


@functools.lru_cache(maxsize=None)
def _small_dft_tables(L):
    nf = L + 1
    nfp = -(-nf // SUB) * SUB
    n = np.arange(L)
    k = np.arange(nf)
    ang = 2 * np.pi * np.outer(k, n) / (2 * L)
    fwd = np.zeros((2 * nfp, L))
    fwd[:nf] = np.cos(ang)
    fwd[nfp:nfp + nf] = -np.sin(ang)
    ck = np.where((k == 0) | (k == L), 1.0, 2.0)
    inv = np.zeros((L, 2 * nfp))
    inv[:, :nf] = np.cos(ang).T * ck / (2 * L)
    inv[:, nfp:nfp + nf] = -np.sin(ang).T * ck / (2 * L)
    return fwd.astype(np.float32), inv.astype(np.float32)


@functools.lru_cache(maxsize=None)
def _filter_features(L):
    t = np.linspace(0.0, 1.0, L)[:, None]
    bands = (FILT_EMB - 1) // 2
    f = np.linspace(1e-4, bands - 1, bands)[None, :]
    wpos = (2.0 * math.pi) * np.arange(L)[:, None] / L
    z = np.concatenate([t, np.cos(f * wpos), -np.sin(f * wpos)], axis=-1)
    return z.astype(np.float32)


@functools.lru_cache(maxsize=None)
def _filter_deltas():
    d = np.linspace(math.log(FILT_TARGET) / FILT_FAST_DECAY,
                    math.log(FILT_TARGET) / FILT_SLOW_DECAY, HY_W)
    return np.abs(d)[None, :].astype(np.float32)


@functools.lru_cache(maxsize=None)
def _rope_table(n):
    rows = n // GRID_W
    row = np.repeat(np.arange(rows, dtype=np.float64), GRID_W)
    col = np.tile(np.arange(GRID_W, dtype=np.float64), rows)
    nf = D_ROPE // 4
    inv = ROPE_THETA ** (-np.arange(nf, dtype=np.float64) / nf)
    ang = np.stack([row[:, None] * inv, col[:, None] * inv], axis=1)
    cos = np.broadcast_to(np.cos(ang)[:, :, None, :], (n, 2, 2, nf)).reshape(n, D_ROPE)
    sin = np.broadcast_to(np.sin(ang)[:, :, None, :], (n, 2, 2, nf)).reshape(n, D_ROPE)
    return np.concatenate([cos, sin], -1).astype(np.float32)


def _identity_rope_table(n):
    return np.concatenate([np.ones((n, D_ROPE), np.float32), np.zeros((n, D_ROPE), np.float32)], -1)


def _rot_half_cols(w):
    w4 = w.reshape(w.shape[:-1] + (2, 2, D_ROPE // 4))
    return jnp.stack([-w4[..., 1, :], w4[..., 0, :]], axis=-2).reshape(w.shape)


def _mod_kernel(c_ref, w_ref, b_ref, o_ref):
    c = c_ref[...]
    s = c * jax.nn.sigmoid(c)
    o_ref[0] = jnp.dot(s, w_ref[0], preferred_element_type=F32) + b_ref[0]


def _modulation(cvec, ada_w, ada_b):
    depth, d, n = ada_w.shape
    r = cvec.shape[0]
    tn = 512
    return pl.pallas_call(
        _mod_kernel,
        grid=(depth, n // tn),
        in_specs=[pl.BlockSpec((r, d), lambda l, j: (0, 0)),
                  pl.BlockSpec((1, d, tn), lambda l, j: (l, 0, j)),
                  pl.BlockSpec((1, 1, tn), lambda l, j: (l, 0, j))],
        out_specs=pl.BlockSpec((1, r, tn), lambda l, j: (l, 0, j)),
        out_shape=jax.ShapeDtypeStruct((depth, r, n), F32),
        compiler_params=_cparams("arbitrary", "arbitrary"),
        name="adaln_modulation",
    )(cvec, ada_w, ada_b.reshape(depth, 1, n))


_NT = (((1,), (1,)), ((), ()))


def _prenorm(x_ref, sc_ref, sh_ref, g_ref):
    x = x_ref[...]
    r = lax.rsqrt(jnp.mean(x * x, axis=-1, keepdims=True) + EPS)
    return ((x * r) * (g_ref[...] * (1.0 + sc_ref[0])) + sh_ref[0]).astype(BF16)


def _win_kernel(x_ref, sc_ref, sh_ref, g_ref, wh_ref, wt_ref, o_ref, hx_ref):
    j = pl.program_id(1)
    tn = o_ref.shape[1]

    @pl.when(j == 0)
    def _():
        hx_ref[...] = _prenorm(x_ref, sc_ref, sh_ref, g_ref)
        o_ref[...] = lax.dot_general(hx_ref[...], wh_ref[...], _NT, preferred_element_type=F32).astype(o_ref.dtype)

    @pl.when(j > 0)
    def _():
        row = pl.multiple_of(BULK_ROW0 + (j - 1) * tn, D_ROPE)
        o_ref[...] = lax.dot_general(hx_ref[...], wt_ref[0, pl.ds(row, tn), :], _NT,
                                     preferred_element_type=F32).astype(o_ref.dtype)


def _input_proj(x2d, sc, sh, pre_g, w_head, wt_all, layer, rows_per_mod, tm):
    m, d = x2d.shape
    tn = P_GM
    per = rows_per_mod // tm
    n_in = wt_all.shape[1]
    assert (N_PROJ - tn) == n_in - BULK_ROW0
    return pl.pallas_call(
        _win_kernel,
        grid=(m // tm, N_PROJ // tn),
        in_specs=[pl.BlockSpec((tm, d), lambda i, j: (i, 0)),
                  pl.BlockSpec((1, 1, d), lambda i, j: (i // per, 0, 0)),
                  pl.BlockSpec((1, 1, d), lambda i, j: (i // per, 0, 0)),
                  pl.BlockSpec((1, d), lambda i, j: (0, 0)),
                  _const_spec((tn, d)),
                  pl.BlockSpec((1, n_in, d), lambda i, j: (layer, 0, 0), pipeline_mode=pl.Buffered(1))],
        out_specs=pl.BlockSpec((tm, tn), lambda i, j: (i, j)),
        out_shape=jax.ShapeDtypeStruct((m, N_PROJ), BF16),
        scratch_shapes=[pltpu.VMEM((tm, d), BF16)],
        compiler_params=_cparams("arbitrary", "arbitrary"),
        name="prenorm_input_proj",
    )(x2d, sc, sh, pre_g, w_head, wt_all)


def _win_kv_kernel(x_ref, sc_ref, sh_ref, g_ref, wh_ref, o_ref):
    hx = _prenorm(x_ref, sc_ref, sh_ref, g_ref)
    o_ref[...] = lax.dot_general(hx, wh_ref[...], _NT, preferred_element_type=F32).astype(o_ref.dtype)


def _input_proj_kv(x2d, sc, sh, pre_g, w_head, tm):
    m, d = x2d.shape
    kw = 2 * KV_LORA
    return pl.pallas_call(
        _win_kv_kernel,
        grid=(m // tm,),
        in_specs=[pl.BlockSpec((tm, d), lambda i: (i, 0)),
                  pl.BlockSpec((1, 1, d), lambda i: (0, 0, 0)),
                  pl.BlockSpec((1, 1, d), lambda i: (0, 0, 0)),
                  pl.BlockSpec((1, d), lambda i: (0, 0)),
                  pl.BlockSpec((kw, d), lambda i: (P_KV // kw, 0))],
        out_specs=pl.BlockSpec((tm, kw), lambda i: (i, 0)),
        out_shape=jax.ShapeDtypeStruct((m, kw), BF16),
        compiler_params=_cparams("arbitrary"),
        name="prenorm_kv_input_proj",
    )(x2d, sc, sh, pre_g, w_head)


def _rope_lanes(v, tab):
    t = v * tab
    return t + pltpu.roll(t, D_ROPE, 1)


def _q_kernel(p_ref, g_ref, w_ref, tab_ref, o_ref):
    p = p_ref[...].astype(F32)
    r = lax.rsqrt(jnp.mean(p * p, axis=-1, keepdims=True) + EPS)
    cq = ((p * r) * g_ref[...]).astype(BF16)
    tab = tab_ref[...]
    for h in range(N_HEADS):
        lo = h * D_QK
        res = jnp.dot(cq, w_ref[:, lo:lo + D_QK], preferred_element_type=F32)
        o_ref[:, lo:lo + D_NOPE] = (res[:, :D_NOPE] * Q_SCALE).astype(o_ref.dtype)
        o_ref[:, lo + D_NOPE:lo + D_QK] = (_rope_lanes(res[:, D_NOPE:], tab) * Q_SCALE).astype(o_ref.dtype)


def _q_proj(px, q_g, wq, tab, rows_per_seq, tm):
    m = px.shape[0]
    per = rows_per_seq // tm
    return pl.pallas_call(
        _q_kernel,
        grid=(m // tm,),
        in_specs=[pl.BlockSpec((tm, Q_LORA), lambda i: (i, P_Q // Q_LORA)),
                  pl.BlockSpec((1, Q_LORA), lambda i: (0, 0)),
                  pl.BlockSpec((Q_LORA, N_HEADS * D_QK), lambda i: (0, 0)),
                  pl.BlockSpec((tm, 2 * D_ROPE), lambda i: (i % per, 0))],
        out_specs=pl.BlockSpec((tm, N_HEADS * D_QK), lambda i: (i, 0)),
        out_shape=jax.ShapeDtypeStruct((m, N_HEADS * D_QK), BF16),
        compiler_params=_cparams("arbitrary"),
        name="mla_q_proj",
    )(px, q_g, wq, tab)


def _kv_kernel(p_ref, g_ref, wk_ref, wvt_ref, tab_ref, k_ref, vt_ref, kn2_ref):
    p = p_ref[:, :KV_LORA].astype(F32)
    r = lax.rsqrt(jnp.mean(p * p, axis=-1, keepdims=True) + EPS)
    ckv = ((p * r) * g_ref[...]).astype(BF16)
    kr = _rope_lanes(p_ref[:, KV_LORA:KV_LORA + 2 * D_ROPE].astype(F32), tab_ref[...])
    lane = lax.broadcasted_iota(jnp.int32, kr.shape, 1)
    kr = jnp.where(lane < D_ROPE, kr, 0.0).astype(k_ref.dtype)
    kn = jnp.dot(ckv, wk_ref[...], preferred_element_type=F32).astype(k_ref.dtype)
    krf = kr.astype(F32)
    kr2 = jnp.sum(krf * krf, axis=-1, keepdims=True)
    norms = []
    for h in range(N_HEADS):
        kh = kn[:, h * D_NOPE:(h + 1) * D_NOPE]
        k_ref[0, :, h * D_QK:h * D_QK + D_NOPE] = kh
        k_ref[0, :, h * D_QK + D_NOPE:(h + 1) * D_QK] = kr
        khf = kh.astype(F32)
        n2 = jnp.max(jnp.sum(khf * khf, axis=-1, keepdims=True) + kr2, axis=0, keepdims=True)
        norms.append(jnp.broadcast_to(n2, (1, LANES)))
    vt = lax.dot_general(wvt_ref[...], ckv, (((1,), (1,)), ((), ())), preferred_element_type=F32)
    vt_ref[0] = vt.astype(vt_ref.dtype)

    norms = jnp.concatenate(norms, axis=0)

    @pl.when(pl.program_id(1) == 0)
    def _():
        kn2_ref[0] = norms

    @pl.when(pl.program_id(1) > 0)
    def _():
        kn2_ref[0] = jnp.maximum(kn2_ref[0], norms)


def _kv_proj(p, col_blk, kv_g, wk, wvt, tab, bsz, ltok, tm):
    per = ltok // tm
    kw = 2 * KV_LORA
    return pl.pallas_call(
        _kv_kernel,
        grid=(bsz, per),
        in_specs=[pl.BlockSpec((tm, kw), lambda b, t: (b * per + t, col_blk)),
                  pl.BlockSpec((1, KV_LORA), lambda b, t: (0, 0)),
                  pl.BlockSpec((KV_LORA, N_HEADS * D_NOPE), lambda b, t: (0, 0)),
                  pl.BlockSpec((MLA_W, KV_LORA), lambda b, t: (0, 0)),
                  pl.BlockSpec((tm, 2 * D_ROPE), lambda b, t: (t, 0))],
        out_specs=[pl.BlockSpec((1, tm, N_HEADS * D_QK), lambda b, t: (b, t, 0)),
                   pl.BlockSpec((1, MLA_W, tm), lambda b, t: (b, 0, t)),
                   pl.BlockSpec((1, N_HEADS, LANES), lambda b, t: (b, 0, 0))],
        out_shape=[jax.ShapeDtypeStruct((bsz, ltok, N_HEADS * D_QK), BF16),
                   jax.ShapeDtypeStruct((bsz, MLA_W, ltok), BF16),
                   jax.ShapeDtypeStruct((bsz, N_HEADS, LANES), F32)],
        compiler_params=_cparams("arbitrary", "arbitrary"),
        name="mla_kv_proj",
    )(p, kv_g, wk, wvt, tab)


def _attn_kernel(q_ref, *refs, n_sub, n_kv):
    k_refs, vt_refs, kn2_refs = refs[0:3 * n_kv:3], refs[1:3 * n_kv:3], refs[2:3 * n_kv:3]
    o_ref, s_ref = refs[3 * n_kv:]
    rows = q_ref.shape[1] // n_sub
    offs = [0]
    for k_ref in k_refs:
        offs.append(offs[-1] + k_ref.shape[1])

    h = pl.program_id(1)
    k2 = functools.reduce(jnp.maximum, [r[0, pl.ds(h, 1), :] for r in kn2_refs])
    k2 = jnp.max(k2, axis=1, keepdims=True)
    qf = q_ref[0].astype(F32)
    q2 = lax.dot_general(jnp.ones((SUB, D_QK), BF16), (qf * qf).astype(BF16), _NT,
                         preferred_element_type=F32)[0:1]
    bound = jnp.sqrt(q2 * k2) * BOUND_SLACK
    safe = jnp.max(bound) <= SAFE_BOUND

    def finish_out(t, ot, l):
        o_ref[0, t * rows:(t + 1) * rows, :] = (ot / l).T.astype(o_ref.dtype)

    @pl.when(safe)
    def _():
        frows = min(FAST_ROWS, q_ref.shape[1])
        for r0 in range(0, q_ref.shape[1], frows):
            q = q_ref[0, r0:r0 + frows, :]
            b = bound[:, r0:r0 + frows]
            l = jnp.zeros((1, frows), F32)
            ot = jnp.zeros((D_V, frows), F32)
            for k_ref, vt_ref in zip(k_refs, vt_refs):
                pt = jnp.exp2(lax.dot_general(k_ref[0], q, _NT, preferred_element_type=F32) - b)
                l = l + jnp.sum(pt, axis=0, keepdims=True)
                ot = ot + jnp.dot(vt_ref[0], pt.astype(BF16), preferred_element_type=F32)
            o_ref[0, r0:r0 + frows, :] = (ot / l).T.astype(o_ref.dtype)

    @pl.when(jnp.logical_not(safe))
    def _():
        def scores(t):
            q = q_ref[0, t * rows:(t + 1) * rows, :]
            for i, k_ref in enumerate(k_refs):
                s_ref[t % 2, offs[i]:offs[i + 1]] = lax.dot_general(k_ref[0], q, _NT, preferred_element_type=F32)

        def finish(t):
            st = s_ref[t % 2]
            m = jnp.max(st, axis=0, keepdims=True)
            pt = jnp.exp2(st - m)
            l = jnp.sum(pt, axis=0, keepdims=True)
            pt = pt.astype(BF16)
            ot = sum(jnp.dot(vt_ref[0], pt[offs[i]:offs[i + 1]], preferred_element_type=F32)
                     for i, vt_ref in enumerate(vt_refs))
            finish_out(t, ot, l)

        scores(0)
        for t in range(n_sub):
            if t + 1 < n_sub:
                scores(t + 1)
            finish(t)


def _attention(q, kv_sets, tq, n_sub=1):
    bsz, lq, _ = q.shape
    in_specs = [pl.BlockSpec((1, tq, D_QK), lambda b, h, i: (b, i, h))]
    args = [q]
    for k, vt, kn2 in kv_sets:
        lk = k.shape[1]
        in_specs += [pl.BlockSpec((1, lk, D_QK), lambda b, h, i: (b, 0, h)),
                     pl.BlockSpec((1, D_V, lk), lambda b, h, i: (b, h, 0)),
                     pl.BlockSpec((1, N_HEADS, LANES), lambda b, h, i: (b, 0, 0))]
        args += [k, vt, kn2]
    lk_all = sum(kv[0].shape[1] for kv in kv_sets)
    return pl.pallas_call(
        functools.partial(_attn_kernel, n_sub=n_sub, n_kv=len(kv_sets)),
        grid=(bsz, N_HEADS, lq // tq),
        in_specs=in_specs,
        out_specs=pl.BlockSpec((1, tq, D_V), lambda b, h, i: (b, i, h)),
        out_shape=jax.ShapeDtypeStruct((bsz, lq, MLA_W), BF16),
        scratch_shapes=[pltpu.VMEM((2, lk_all, tq // n_sub), F32)],
        compiler_params=_cparams("arbitrary", "arbitrary", "arbitrary"),
        name="mla_attention",
    )(*args)


def _filter_mlp(z_ref, w1_ref, b1_ref, fr_ref, w2_ref, b2_ref):
    hi = lax.Precision.HIGHEST
    fr = fr_ref[...]
    h = jnp.sin(fr * (jnp.dot(z_ref[...], w1_ref[...], precision=hi, preferred_element_type=F32) + b1_ref[...]))
    return jnp.sin(fr * (jnp.dot(h, w2_ref[...], precision=hi, preferred_element_type=F32) + b2_ref[...]))


def _filter_taps(h, z_ref, w3_ref, dl_ref, zero_first):
    taps = jnp.dot(h.astype(BF16), w3_ref[...].astype(BF16), preferred_element_type=F32)
    taps = taps * jnp.exp(-z_ref[:, 0:1] * dl_ref[...])
    if zero_first:
        row = lax.broadcasted_iota(jnp.int32, taps.shape, 0)
        taps = jnp.where(row == 0, 0.0, taps)
    return taps


def _fill_padded(pad_ref, src, n):
    ct = pad_ref.shape[1]
    pad_ref[0:SUB, :] = jnp.zeros((SUB, ct), F32)
    pad_ref[SUB + n:2 * SUB + n, :] = jnp.zeros((SUB, ct), F32)
    pad_ref[SUB:SUB + n, :] = src.astype(F32)


def _short_conv(pad_ref, w_ref, b_ref, r0, rows):
    n = rows + 2 * SUB
    win = pad_ref[r0:r0 + n, :]
    prev = pltpu.roll(win, 1, 0)[SUB:SUB + rows]
    nxt = pltpu.roll(win, n - 1, 0)[SUB:SUB + rows]
    return (b_ref[...] + prev * w_ref[0:1, :] + win[SUB:SUB + rows] * w_ref[1:2, :] + nxt * w_ref[2:3, :])


def _cmul(ur, ui, kr, ki):
    return ur * kr - ui * ki, ur * ki + ui * kr


CONV_CHUNK = 512
MID_UNROLL = 33


def _stage1(u_ref, kf_ref, a_ref):
    n1h, _, ct = u_ref.shape
    for j in range(FFT_N2 // SUB):
        slab = u_ref[:, j * SUB:(j + 1) * SUB, :].reshape(n1h * SUB, ct).astype(BF16)
        r = jnp.dot(kf_ref[...], slab, preferred_element_type=F32)
        a_ref[:, :, j * SUB:(j + 1) * SUB, :] = r.reshape(2, FFT_H1, SUB, ct)


def _stage2(a_ref, rb_ref, i):
    ct = a_ref.shape[-1]
    x = a_ref[:, i].reshape(2 * FFT_N2, ct).astype(BF16)
    return jnp.dot(rb_ref[i], x, preferred_element_type=F32)


def _lfilt_kernel(z_ref, w1_ref, b1_ref, fr_ref, w2_ref, b2_ref, w3f_ref, w3b_ref, dl_ref,
                  kf_ref, rb_ref, o_ref, h_ref, u_ref, a_ref, s_ref):
    ct = u_ref.shape[-1]

    @pl.when(pl.program_id(0) == 0)
    def _():
        h_ref[...] = _filter_mlp(z_ref, w1_ref, b1_ref, fr_ref, w2_ref, b2_ref)

    for w3_ref, back in ((w3f_ref, False), (w3b_ref, True)):
        taps = _filter_taps(h_ref[...], z_ref, w3_ref, dl_ref, zero_first=back)
        u_ref[...] = taps.reshape(u_ref.shape)
        _stage1(u_ref, kf_ref, a_ref)

        def body(i, carry, back=back):
            s = _stage2(a_ref, rb_ref, i).reshape(2, FFT_N2, ct)
            if back:
                o_ref[0, i] = (s_ref[0, i] + s[0]).astype(o_ref.dtype)
                o_ref[1, i] = (s_ref[1, i] - s[1]).astype(o_ref.dtype)
            else:
                s_ref[:, i] = s
            return carry

        lax.fori_loop(0, FFT_H1, body, 0, unroll=MID_UNROLL)


def _long_filter_spectrum(lp, L, ct):
    kron_f, _, rb, _ = _dft_tables()
    z = jnp.asarray(_filter_features(L))
    dl = jnp.asarray(_filter_deltas())
    nct = HY_W // ct
    full = lambda shape: pl.BlockSpec(shape, lambda j: (0,) * len(shape))
    return pl.pallas_call(
        _lfilt_kernel,
        grid=(nct,),
        in_specs=[full((L, FILT_EMB)), full((FILT_EMB, FILT_HIDDEN)), full((1, FILT_HIDDEN)),
                  full((1, FILT_HIDDEN)), full((FILT_HIDDEN, FILT_HIDDEN)), full((1, FILT_HIDDEN)),
                  pl.BlockSpec((FILT_HIDDEN, ct), lambda j: (0, j)),
                  pl.BlockSpec((FILT_HIDDEN, ct), lambda j: (0, j + nct)),
                  pl.BlockSpec((1, ct), lambda j: (0, j)),
                  _const_spec(kron_f.shape), _const_spec(rb.shape)],
        out_specs=pl.BlockSpec((2, FFT_H1, FFT_N2, ct), lambda j: (0, 0, 0, j)),
        out_shape=jax.ShapeDtypeStruct((2, FFT_H1, FFT_N2, HY_W), BF16),
        scratch_shapes=[pltpu.VMEM((L, FILT_HIDDEN), F32),
                        pltpu.VMEM((FFT_N1 // 2, FFT_N2, ct), F32),
                        pltpu.VMEM((2, FFT_H1, FFT_N2, ct), F32),
                        pltpu.VMEM((2, FFT_H1, FFT_N2, ct), F32)],
        compiler_params=_cparams("arbitrary"),
        name="hyena_filter_spectrum",
    )(z, lp["filt_w1"], lp["filt_b1"][None], lp["filt_freq"][None], lp["filt_w2"], lp["filt_b2"][None],
      lp["filt_w3"], lp["filt_w3"], dl, jnp.asarray(kron_f).astype(BF16), jnp.asarray(rb).astype(BF16))


def _lhyena_kernel(x0_ref, x1_ref, v_ref, w0_ref, w1_ref, wv_ref, b0_ref, b1_ref, bv_ref, d_ref,
                   ks_ref, kf_ref, ki_ref, rb_ref, rbt_ref, o_ref, pad_ref, u_ref, a_ref):
    n1h, n2, ct = u_ref.shape
    L = n1h * n2
    nchunk = L // CONV_CHUNK
    cpt = CONV_CHUNK // n2

    _fill_padded(pad_ref, x1_ref[0].reshape(L, ct), L)
    for c in range(nchunk):
        u_ref[c * cpt:(c + 1) * cpt] = _short_conv(pad_ref, w1_ref, b1_ref, c * CONV_CHUNK, CONV_CHUNK).reshape(cpt, n2, ct)
    _fill_padded(pad_ref, v_ref[0].reshape(L, ct), L)
    for c in range(nchunk):
        vv = _short_conv(pad_ref, wv_ref, bv_ref, c * CONV_CHUNK, CONV_CHUNK).reshape(cpt, n2, ct)
        u_ref[c * cpt:(c + 1) * cpt] = u_ref[c * cpt:(c + 1) * cpt] * vv

    _stage1(u_ref, kf_ref, a_ref)

    def body(i, carry):
        uh = _stage2(a_ref, rb_ref, i)
        kr = ks_ref[0, i].astype(F32)
        ki = ks_ref[1, i].astype(F32)
        zr, zi = _cmul(uh[:n2], uh[n2:], kr, ki)
        zz = jnp.concatenate([zr, zi], axis=0).astype(BF16)
        g = jnp.dot(rbt_ref[i], zz, preferred_element_type=F32)
        a_ref[:, i] = g.reshape(2, n2, ct)
        return carry

    lax.fori_loop(0, FFT_H1, body, 0, unroll=MID_UNROLL)

    for j in range(n2 // SUB):
        g = jnp.concatenate([
            a_ref[0, :, j * SUB:(j + 1) * SUB, :].reshape(FFT_H1 * SUB, ct),
            a_ref[1, 1:FFT_H1 - 1, j * SUB:(j + 1) * SUB, :].reshape((FFT_H1 - 2) * SUB, ct)], axis=0).astype(BF16)
        y = jnp.dot(ki_ref[...], g, preferred_element_type=F32).reshape(n1h, SUB, ct)
        o_ref[0, :, j * SUB:(j + 1) * SUB, :] = y + u_ref[:, j * SUB:(j + 1) * SUB, :] * d_ref[...]

    _fill_padded(pad_ref, x0_ref[0].reshape(L, ct), L)
    for c in range(nchunk):
        x0 = _short_conv(pad_ref, w0_ref, b0_ref, c * CONV_CHUNK, CONV_CHUNK).reshape(cpt, n2, ct)
        o_ref[0, c * cpt:(c + 1) * cpt] = o_ref[0, c * cpt:(c + 1) * cpt] * x0


def _long_hyena(px, lp, kspec, bsz, L, ct):
    kron_f, kron_i, rb, rbt = _dft_tables()
    n1h = L // FFT_N2
    px4 = px.reshape(bsz, n1h, FFT_N2, N_PROJ)
    nct = HY_W // ct
    c0 = P_HY // ct
    part = lambda k: pl.BlockSpec((1, n1h, FFT_N2, ct), lambda j, b: (b, 0, 0, c0 + k * nct + j))
    wpart = lambda k: pl.BlockSpec((3, ct), lambda j, b: (0, k * nct + j))
    bpart = lambda k: pl.BlockSpec((1, ct), lambda j, b: (0, k * nct + j))
    cb = lp["conv_b"][None]
    out = pl.pallas_call(
        _lhyena_kernel,
        grid=(nct, bsz),
        in_specs=[part(0), part(1), part(2), wpart(0), wpart(1), wpart(2), bpart(0), bpart(1), bpart(2),
                  pl.BlockSpec((1, ct), lambda j, b: (0, j)),
                  pl.BlockSpec((2, FFT_H1, FFT_N2, ct), lambda j, b: (0, 0, 0, j), pipeline_mode=pl.Buffered(1)),
                  _const_spec(kron_f.shape), _const_spec(kron_i.shape),
                  _const_spec(rb.shape), _const_spec(rbt.shape)],
        out_specs=pl.BlockSpec((1, n1h, FFT_N2, ct), lambda j, b: (b, 0, 0, j)),
        out_shape=jax.ShapeDtypeStruct((bsz, n1h, FFT_N2, HY_W), F32),
        scratch_shapes=[pltpu.VMEM((L + 2 * SUB, ct), F32),
                        pltpu.VMEM((n1h, FFT_N2, ct), F32),
                        pltpu.VMEM((2, FFT_H1, FFT_N2, ct), F32)],
        compiler_params=_cparams("arbitrary", "arbitrary"),
        name="hyena_long_conv",
    )(px4, px4, px4, lp["conv_w"], lp["conv_w"], lp["conv_w"], cb, cb, cb, lp["hy_D"][None], kspec,
      jnp.asarray(kron_f).astype(BF16), jnp.asarray(kron_i).astype(BF16),
      jnp.asarray(rb).astype(BF16), jnp.asarray(rbt).astype(BF16))
    return out.reshape(bsz * L, HY_W)


def _sfilt_kernel(z_ref, w1_ref, b1_ref, fr_ref, w2_ref, b2_ref, w3f_ref, w3b_ref, dl_ref, f_ref, o_ref):
    h = _filter_mlp(z_ref, w1_ref, b1_ref, fr_ref, w2_ref, b2_ref)
    nfp = f_ref.shape[0] // 2
    sf = jnp.dot(f_ref[...], _filter_taps(h, z_ref, w3f_ref, dl_ref, False).astype(BF16), preferred_element_type=F32)
    sb = jnp.dot(f_ref[...], _filter_taps(h, z_ref, w3b_ref, dl_ref, True).astype(BF16), preferred_element_type=F32)
    o_ref[0:nfp, :] = sf[:nfp] + sb[:nfp]
    o_ref[nfp:, :] = sf[nfp:] - sb[nfp:]


def _short_filter_spectrum(lp, L, ct):
    fwd, _ = _small_dft_tables(L)
    z = jnp.asarray(_filter_features(L))
    dl = jnp.asarray(_filter_deltas())
    nct = HY_W // ct
    full = lambda shape: pl.BlockSpec(shape, lambda j: (0,) * len(shape))
    return pl.pallas_call(
        _sfilt_kernel,
        grid=(nct,),
        in_specs=[full((L, FILT_EMB)), full((FILT_EMB, FILT_HIDDEN)), full((1, FILT_HIDDEN)),
                  full((1, FILT_HIDDEN)), full((FILT_HIDDEN, FILT_HIDDEN)), full((1, FILT_HIDDEN)),
                  pl.BlockSpec((FILT_HIDDEN, ct), lambda j: (0, j)),
                  pl.BlockSpec((FILT_HIDDEN, ct), lambda j: (0, j + nct)),
                  pl.BlockSpec((1, ct), lambda j: (0, j)),
                  full(fwd.shape)],
        out_specs=pl.BlockSpec((fwd.shape[0], ct), lambda j: (0, j)),
        out_shape=jax.ShapeDtypeStruct((fwd.shape[0], HY_W), F32),
        compiler_params=_cparams("arbitrary"),
        name="hyena_ctx_filter_spectrum",
    )(z, lp["filt_w1"], lp["filt_b1"][None], lp["filt_freq"][None], lp["filt_w2"], lp["filt_b2"][None],
      lp["filt_w3"], lp["filt_w3"], dl, jnp.asarray(fwd).astype(BF16))


def _shyena_kernel(x0_ref, x1_ref, v_ref, w0_ref, w1_ref, wv_ref, b0_ref, b1_ref, bv_ref, d_ref,
                   ks_ref, f_ref, fi_ref, o_ref, pad_ref):
    L = x0_ref.shape[0]
    nfp = f_ref.shape[0] // 2
    _fill_padded(pad_ref, x1_ref[...], L)
    u = _short_conv(pad_ref, w1_ref, b1_ref, 0, L)
    _fill_padded(pad_ref, v_ref[...], L)
    u = u * _short_conv(pad_ref, wv_ref, bv_ref, 0, L)
    uh = jnp.dot(f_ref[...], u.astype(BF16), preferred_element_type=F32)
    zr, zi = _cmul(uh[:nfp], uh[nfp:], ks_ref[0:nfp, :], ks_ref[nfp:, :])
    zz = jnp.concatenate([zr, zi], axis=0).astype(BF16)
    y = jnp.dot(fi_ref[...], zz, preferred_element_type=F32)
    _fill_padded(pad_ref, x0_ref[...], L)
    o_ref[...] = _short_conv(pad_ref, w0_ref, b0_ref, 0, L) * (y + u * d_ref[...])


def _short_hyena(pc, lp, kspec, bsz, L, ct):
    fwd, inv = _small_dft_tables(L)
    nct = HY_W // ct
    c0 = P_HY // ct
    part = lambda k: pl.BlockSpec((L, ct), lambda j, b: (b, c0 + k * nct + j))
    wpart = lambda k: pl.BlockSpec((3, ct), lambda j, b: (0, k * nct + j))
    bpart = lambda k: pl.BlockSpec((1, ct), lambda j, b: (0, k * nct + j))
    full = lambda shape: pl.BlockSpec(shape, lambda j, b: (0,) * len(shape))
    cb = lp["conv_b"][None]
    return pl.pallas_call(
        _shyena_kernel,
        grid=(nct, bsz),
        in_specs=[part(0), part(1), part(2), wpart(0), wpart(1), wpart(2), bpart(0), bpart(1), bpart(2),
                  pl.BlockSpec((1, ct), lambda j, b: (0, j)),
                  pl.BlockSpec((fwd.shape[0], ct), lambda j, b: (0, j)),
                  full(fwd.shape), full(inv.shape)],
        out_specs=pl.BlockSpec((L, ct), lambda j, b: (b, j)),
        out_shape=jax.ShapeDtypeStruct((bsz * L, HY_W), F32),
        scratch_shapes=[pltpu.VMEM((L + 2 * SUB, ct), F32)],
        compiler_params=_cparams("arbitrary", "arbitrary"),
        name="hyena_ctx_conv",
    )(pc, pc, pc, lp["conv_w"], lp["conv_w"], lp["conv_w"], cb, cb, cb, lp["hy_D"][None], kspec,
      jnp.asarray(fwd).astype(BF16), jnp.asarray(inv).astype(BF16))


def _merge_kernel(o_ref, gm_ref, y_ref, gh_ref, x_ref, gx_ref, gmla_ref, ghy_ref, pg_ref, w_ref, out_ref):
    def normed(t, g_ref, gate_ref):
        r = lax.rsqrt(jnp.mean(t * t, axis=-1, keepdims=True) + EPS)
        gate = gate_ref[...].astype(F32)
        return ((t * r) * g_ref[...] * (gate * jax.nn.sigmoid(gate))).astype(BF16)

    a = normed(o_ref[...].astype(F32), gmla_ref, gm_ref)
    b = normed(y_ref[...], ghy_ref, gh_ref)
    z = (jnp.dot(a, w_ref[:MLA_W, :], preferred_element_type=F32)
         + jnp.dot(b, w_ref[MLA_W:, :], preferred_element_type=F32))
    r = lax.rsqrt(jnp.mean(z * z, axis=-1, keepdims=True) + EPS)
    out_ref[...] = x_ref[...] + gx_ref[0] * ((z * r) * pg_ref[...])


def _merge(o, px, yh, x2d, gx, lp, w_out, rows_per_mod, tm):
    m, d = x2d.shape
    per = rows_per_mod // tm
    return pl.pallas_call(
        _merge_kernel,
        grid=(m // tm,),
        in_specs=[pl.BlockSpec((tm, MLA_W), lambda i: (i, 0)),
                  pl.BlockSpec((tm, MLA_W), lambda i: (i, P_GM // MLA_W)),
                  pl.BlockSpec((tm, HY_W), lambda i: (i, 0)),
                  pl.BlockSpec((tm, HY_W), lambda i: (i, P_GH // HY_W)),
                  pl.BlockSpec((tm, d), lambda i: (i, 0)),
                  pl.BlockSpec((1, 1, d), lambda i: (i // per, 0, 0)),
                  pl.BlockSpec((1, MLA_W), lambda i: (0, 0)),
                  pl.BlockSpec((1, HY_W), lambda i: (0, 0)),
                  pl.BlockSpec((1, d), lambda i: (0, 0)),
                  _const_spec((MLA_W + HY_W, d))],
        out_specs=pl.BlockSpec((tm, d), lambda i: (i, 0)),
        out_shape=jax.ShapeDtypeStruct((m, d), F32),
        compiler_params=_cparams("arbitrary"),
        name="branch_merge_out_proj",
    )(o, px, yh, px, x2d, gx, lp["grp_g_mla"][None], lp["grp_g_hy"][None], lp["post_g"][None], w_out)


def _head_rows(wt_all, layer):
    off_kr = Q_LORA + KV_LORA
    head = wt_all[layer, :BULK_ROW0]
    kr_rot = _rot_half_cols(head[off_kr:].T).T
    pad = jnp.zeros((P_GM - BULK_ROW0 - D_ROPE, head.shape[1]), head.dtype)
    return jnp.concatenate([head, kr_rot, pad], axis=0)


def _prep_weights(lp, wt_all, layer):
    w_head = _head_rows(wt_all, layer)
    wq = lp["w_uq"].reshape(Q_LORA, N_HEADS, D_NOPE + D_ROPE)
    wq_r = wq[..., D_NOPE:]
    wq_cat = jnp.concatenate([wq[..., :D_NOPE], wq_r, _rot_half_cols(wq_r)], axis=-1)
    wq_cat = wq_cat.reshape(Q_LORA, N_HEADS * D_QK).astype(BF16)
    wkv = lp["w_ukv"].reshape(KV_LORA, N_HEADS, D_NOPE + D_V)
    wk = wkv[..., :D_NOPE].reshape(KV_LORA, N_HEADS * D_NOPE).astype(BF16)
    wvt = wkv[..., D_NOPE:].reshape(KV_LORA, MLA_W).T.astype(BF16)
    return w_head, wq_cat, wk, wvt, lp["w_out"].astype(BF16)


def _layer(x2d, c2d, mod, lp, wt_all, layer, bsz, L, Lc, update_ctx):
    d = D_MODEL
    w_head, wq_cat, wk, wvt, w_out = _prep_weights(lp, wt_all, layer)
    sh, sc, gt = mod[:, :d], mod[:, d:2 * d], mod[:, 2 * d:]
    sh_x, sc_x, g_x = (t[:bsz, None, :] for t in (sh, sc, gt))
    sh_c, sc_c, g_c = (t[bsz:bsz + 1, None, :] for t in (sh, sc, gt))
    pre_g = lp["pre_g"][None]

    px = _input_proj(x2d, sc_x, sh_x, pre_g, w_head, wt_all, layer, rows_per_mod=L, tm=TM_WIN)
    kv_blk = P_KV // (2 * KV_LORA)
    ctx_tm = min(TM_WIN, bsz * Lc)
    if update_ctx:
        pc = _input_proj(c2d, sc_c, sh_c, pre_g, w_head, wt_all, layer, rows_per_mod=bsz * Lc, tm=ctx_tm)
        pc_kv, pc_blk = pc, kv_blk
    else:
        pc_kv = _input_proj_kv(c2d, sc_c, sh_c, pre_g, w_head, tm=ctx_tm)
        pc_blk = 0

    tab_x = jnp.asarray(_rope_table(L))
    tab_c = jnp.asarray(_identity_rope_table(Lc))
    kv_g = lp["kv_norm_g"][None]
    kv_x = _kv_proj(px, kv_blk, kv_g, wk, wvt, tab_x, bsz, L, tm=TM_QKV)
    kv_c = _kv_proj(pc_kv, pc_blk, kv_g, wk, wvt, tab_c, bsz, Lc, tm=Lc)
    q_x = _q_proj(px, lp["q_norm_g"][None], wq_cat, tab_x, rows_per_seq=L, tm=TM_QKV)
    o_x = _attention(q_x.reshape(bsz, L, -1), [kv_x, kv_c], tq=TQ_ATTN, n_sub=ATTN_SUB).reshape(bsz * L, MLA_W)

    kspec = _long_filter_spectrum(lp, L, ct=CT_HYENA)
    y_x = _long_hyena(px, lp, kspec, bsz, L, ct=CT_HYENA)
    x_new = _merge(o_x, px, y_x, x2d, g_x, lp, w_out, rows_per_mod=L, tm=TM_MERGE)

    c_new = c2d
    if update_ctx:
        q_c = _q_proj(pc, lp["q_norm_g"][None], wq_cat, tab_c, rows_per_seq=Lc, tm=Lc)
        o_c = _attention(q_c.reshape(bsz, Lc, -1), [kv_c], tq=Lc).reshape(bsz * Lc, MLA_W)
        kspec_c = _short_filter_spectrum(lp, Lc, ct=CT_HYENA)
        y_c = _short_hyena(pc, lp, kspec_c, bsz, Lc, ct=CT_HYENA)
        c_new = _merge(o_c, pc, y_c, c2d, g_c, lp, w_out, rows_per_mod=bsz * Lc, tm=min(TM_MERGE, bsz * Lc))
    return x_new, c_new


def kernel(x, c, ctx, c_ctx, ada_w, ada_b, pre_g, w_in, q_norm_g, w_uq, kv_norm_g, w_ukv, conv_w, conv_b,
           filt_w1, filt_b1, filt_freq, filt_w2, filt_b2, filt_w3, hy_D, grp_g_mla, grp_g_hy, w_out, post_g):
    bsz, L, d = x.shape
    Lc = ctx.shape[1]
    depth = ada_w.shape[0]
    assert d == D_MODEL and L == (FFT_N1 // 2) * FFT_N2 and bsz + 1 <= SUB
    cvec = jnp.concatenate([c, c_ctx[None], jnp.zeros((SUB - bsz - 1, d), F32)], axis=0)
    mod = _modulation(cvec, ada_w, ada_b)
    wt_all = jnp.swapaxes(w_in, 1, 2).astype(BF16)
    params = dict(pre_g=pre_g, q_norm_g=q_norm_g, w_uq=w_uq, kv_norm_g=kv_norm_g, w_ukv=w_ukv,
                  conv_w=conv_w, conv_b=conv_b, filt_w1=filt_w1, filt_b1=filt_b1, filt_freq=filt_freq,
                  filt_w2=filt_w2, filt_b2=filt_b2, filt_w3=filt_w3, hy_D=hy_D, grp_g_mla=grp_g_mla,
                  grp_g_hy=grp_g_hy, w_out=w_out, post_g=post_g)
    x2d = x.reshape(bsz * L, d)
    c2d = ctx.reshape(bsz * Lc, d)
    for l in range(depth):
        lp = {k: v[l] for k, v in params.items()}
        x2d, c2d = _layer(x2d, c2d, mod[l], lp, wt_all, l, bsz, L, Lc, update_ctx=(l < depth - 1))
    return x2d.reshape(bsz, L, d)
```

```python
import functools
import math

import numpy as np
import jax
import jax.numpy as jnp
from jax import lax
from jax.experimental import pallas as pl
from jax.experimental.pallas import tpu as pltpu

F32 = jnp.float32
BF16 = jnp.bfloat16

D_MODEL = 2048
GRID_W = 64
N_HEADS = 8
D_NOPE = 128
D_ROPE = 64
D_V = 128
MLA_W = N_HEADS * D_V
Q_LORA = 512
KV_LORA = 256
ROPE_THETA = 10000.0
SCALE = (D_NOPE + D_ROPE) ** -0.5
Q_SCALE = SCALE * math.log2(math.e)
HY_W = 1024
FILT_EMB = 33
FILT_HIDDEN = 64
FILT_TARGET = 1e-2
FILT_FAST_DECAY = 0.3
FILT_SLOW_DECAY = 1.5
EPS = 1e-6

P_Q = 0
P_KV = 512
P_GM = 1024
P_HY = 2048
P_GH = 5120
N_PROJ = 6144
N_IN = Q_LORA + KV_LORA + D_ROPE + MLA_W + 4 * HY_W
BULK_ROW0 = Q_LORA + KV_LORA + D_ROPE
D_QK = 256

VMEM_LIMIT = 58 * 1024 * 1024

FFT_N1 = 64
FFT_N2 = 128
FFT_H1 = FFT_N1 // 2 + 1
SUB = 8
LANES = 128

SAFE_BOUND = 40.0
BOUND_SLACK = 1.03

TM_WIN = 1024
TM_QKV = 512
TQ_ATTN = 4096
ATTN_SUB = 8
FAST_ROWS = 2048
CT_HYENA = 256
TM_MERGE = 512


def _cparams(*sem):
    return pltpu.CompilerParams(dimension_semantics=sem, vmem_limit_bytes=VMEM_LIMIT)


def _const_spec(shape):
    nd = len(shape)
    return pl.BlockSpec(shape, lambda *_: (0,) * nd, pipeline_mode=pl.Buffered(1))


@functools.lru_cache(maxsize=None)
def _dft_tables():
    n1 = np.arange(FFT_N1 // 2)
    k1 = np.arange(FFT_H1)
    eye = np.eye(SUB)
    ang = 2 * np.pi * np.outer(k1, n1) / FFT_N1
    kron_f = np.concatenate([np.kron(np.cos(ang), eye), np.kron(-np.sin(ang), eye)], 0)
    ck = np.where((k1 == 0) | (k1 == FFT_N1 // 2), 1.0, 2.0)
    angi = 2 * np.pi * np.outer(n1, k1) / FFT_N1
    kron_i = np.concatenate([np.kron(np.cos(angi) * ck, eye),
                             np.kron((-np.sin(angi) * ck)[:, 1:FFT_H1 - 1], eye)], 1) / (FFT_N1 * FFT_N2)
    n2 = np.arange(FFT_N2)
    kk = FFT_N1 * np.arange(FFT_N2)[None, :, None] + k1[:, None, None]
    a = 2 * np.pi * kk * n2[None, None, :] / (FFT_N1 * FFT_N2)
    mr, mi = np.cos(a), -np.sin(a)
    rb = np.concatenate([np.concatenate([mr, -mi], 2), np.concatenate([mi, mr], 2)], 1)
    rbt = np.transpose(rb, (0, 2, 1))
    return (kron_f.astype(np.float32), kron_i.astype(np.float32),
            rb.astype(np.float32), rbt.astype(np.float32))


@functools.lru_cache(maxsize=None)
def _small_dft_tables(L):
    nf = L + 1
    nfp = -(-nf // SUB) * SUB
    n = np.arange(L)
    k = np.arange(nf)
    ang = 2 * np.pi * np.outer(k, n) / (2 * L)
    fwd = np.zeros((2 * nfp, L))
    fwd[:nf] = np.cos(ang)
    fwd[nfp:nfp + nf] = -np.sin(ang)
    ck = np.where((k == 0) | (k == L), 1.0, 2.0)
    inv = np.zeros((L, 2 * nfp))
    inv[:, :nf] = np.cos(ang).T * ck / (2 * L)
    inv[:, nfp:nfp + nf] = -np.sin(ang).T * ck / (2 * L)
    return fwd.astype(np.float32), inv.astype(np.float32)


@functools.lru_cache(maxsize=None)
def _filter_features(L):
    t = np.linspace(0.0, 1.0, L)[:, None]
    bands = (FILT_EMB - 1) // 2
    f = np.linspace(1e-4, bands - 1, bands)[None, :]
    wpos = (2.0 * math.pi) * np.arange(L)[:, None] / L
    z = np.concatenate([t, np.cos(f * wpos), -np.sin(f * wpos)], axis=-1)
    return z.astype(np.float32)


@functools.lru_cache(maxsize=None)
def _filter_deltas():
    d = np.linspace(math.log(FILT_TARGET) / FILT_FAST_DECAY,
                    math.log(FILT_TARGET) / FILT_SLOW_DECAY, HY_W)
    return np.abs(d)[None, :].astype(np.float32)


@functools.lru_cache(maxsize=None)
def _rope_table(n):
    rows = n // GRID_W
    row = np.repeat(np.arange(rows, dtype=np.float64), GRID_W)
    col = np.tile(np.arange(GRID_W, dtype=np.float64), rows)
    nf = D_ROPE // 4
    inv = ROPE_THETA ** (-np.arange(nf, dtype=np.float64) / nf)
    ang = np.stack([row[:, None] * inv, col[:, None] * inv], axis=1)
    cos = np.broadcast_to(np.cos(ang)[:, :, None, :], (n, 2, 2, nf)).reshape(n, D_ROPE)
    sin = np.broadcast_to(np.sin(ang)[:, :, None, :], (n, 2, 2, nf)).reshape(n, D_ROPE)
    return np.concatenate([cos, sin], -1).astype(np.float32)


def _identity_rope_table(n):
    return np.concatenate([np.ones((n, D_ROPE), np.float32), np.zeros((n, D_ROPE), np.float32)], -1)


def _rot_half_cols(w):
    w4 = w.reshape(w.shape[:-1] + (2, 2, D_ROPE // 4))
    return jnp.stack([-w4[..., 1, :], w4[..., 0, :]], axis=-2).reshape(w.shape)


def _mod_kernel(c_ref, w_ref, b_ref, o_ref):
    c = c_ref[...]
    s = c * jax.nn.sigmoid(c)
    o_ref[0] = jnp.dot(s, w_ref[0], preferred_element_type=F32) + b_ref[0]


def _modulation(cvec, ada_w, ada_b):
    depth, d, n = ada_w.shape
    r = cvec.shape[0]
    tn = 512
    return pl.pallas_call(
        _mod_kernel,
        grid=(depth, n // tn),
        in_specs=[pl.BlockSpec((r, d), lambda l, j: (0, 0)),
                  pl.BlockSpec((1, d, tn), lambda l, j: (l, 0, j)),
                  pl.BlockSpec((1, 1, tn), lambda l, j: (l, 0, j))],
        out_specs=pl.BlockSpec((1, r, tn), lambda l, j: (l, 0, j)),
        out_shape=jax.ShapeDtypeStruct((depth, r, n), F32),
        compiler_params=_cparams("arbitrary", "arbitrary"),
        name="adaln_modulation",
    )(cvec, ada_w, ada_b.reshape(depth, 1, n))


_NT = (((1,), (1,)), ((), ()))


def _prenorm(x_ref, sc_ref, sh_ref, g_ref):
    x = x_ref[...]
    r = lax.rsqrt(jnp.mean(x * x, axis=-1, keepdims=True) + EPS)
    return ((x * r) * (g_ref[...] * (1.0 + sc_ref[0])) + sh_ref[0]).astype(BF16)


def _win_kernel(x_ref, sc_ref, sh_ref, g_ref, wh_ref, wt_ref, o_ref, hx_ref):
    j = pl.program_id(1)
    tn = o_ref.shape[1]

    @pl.when(j == 0)
    def _():
        hx_ref[...] = _prenorm(x_ref, sc_ref, sh_ref, g_ref)
        o_ref[...] = lax.dot_general(hx_ref[...], wh_ref[...], _NT, preferred_element_type=F32).astype(o_ref.dtype)

    @pl.when(j > 0)
    def _():
        row = pl.multiple_of(BULK_ROW0 + (j - 1) * tn, D_ROPE)
        o_ref[...] = lax.dot_general(hx_ref[...], wt_ref[0, pl.ds(row, tn), :], _NT,
                                     preferred_element_type=F32).astype(o_ref.dtype)


def _input_proj(x2d, sc, sh, pre_g, w_head, wt_all, layer, rows_per_mod, tm):
    m, d = x2d.shape
    tn = P_GM
    per = rows_per_mod // tm
    n_in = wt_all.shape[1]
    assert (N_PROJ - tn) == n_in - BULK_ROW0
    return pl.pallas_call(
        _win_kernel,
        grid=(m // tm, N_PROJ // tn),
        in_specs=[pl.BlockSpec((tm, d), lambda i, j: (i, 0)),
                  pl.BlockSpec((1, 1, d), lambda i, j: (i // per, 0, 0)),
                  pl.BlockSpec((1, 1, d), lambda i, j: (i // per, 0, 0)),
                  pl.BlockSpec((1, d), lambda i, j: (0, 0)),
                  _const_spec((tn, d)),
                  pl.BlockSpec((1, n_in, d), lambda i, j: (layer, 0, 0), pipeline_mode=pl.Buffered(1))],
        out_specs=pl.BlockSpec((tm, tn), lambda i, j: (i, j)),
        out_shape=jax.ShapeDtypeStruct((m, N_PROJ), BF16),
        scratch_shapes=[pltpu.VMEM((tm, d), BF16)],
        compiler_params=_cparams("arbitrary", "arbitrary"),
        name="prenorm_input_proj",
    )(x2d, sc, sh, pre_g, w_head, wt_all)


def _win_kv_kernel(x_ref, sc_ref, sh_ref, g_ref, wh_ref, o_ref):
    hx = _prenorm(x_ref, sc_ref, sh_ref, g_ref)
    o_ref[...] = lax.dot_general(hx, wh_ref[...], _NT, preferred_element_type=F32).astype(o_ref.dtype)


def _input_proj_kv(x2d, sc, sh, pre_g, w_head, tm):
    m, d = x2d.shape
    kw = 2 * KV_LORA
    return pl.pallas_call(
        _win_kv_kernel,
        grid=(m // tm,),
        in_specs=[pl.BlockSpec((tm, d), lambda i: (i, 0)),
                  pl.BlockSpec((1, 1, d), lambda i: (0, 0, 0)),
                  pl.BlockSpec((1, 1, d), lambda i: (0, 0, 0)),
                  pl.BlockSpec((1, d), lambda i: (0, 0)),
                  pl.BlockSpec((kw, d), lambda i: (P_KV // kw, 0))],
        out_specs=pl.BlockSpec((tm, kw), lambda i: (i, 0)),
        out_shape=jax.ShapeDtypeStruct((m, kw), BF16),
        compiler_params=_cparams("arbitrary"),
        name="prenorm_kv_input_proj",
    )(x2d, sc, sh, pre_g, w_head)


def _rope_lanes(v, tab):
    t = v * tab
    return t + pltpu.roll(t, D_ROPE, 1)


def _q_kernel(p_ref, g_ref, w_ref, tab_ref, o_ref):
    p = p_ref[...].astype(F32)
    r = lax.rsqrt(jnp.mean(p * p, axis=-1, keepdims=True) + EPS)
    cq = ((p * r) * g_ref[...]).astype(BF16)
    tab = tab_ref[...]
    for h in range(N_HEADS):
        lo = h * D_QK
        res = jnp.dot(cq, w_ref[:, lo:lo + D_QK], preferred_element_type=F32)
        o_ref[:, lo:lo + D_NOPE] = (res[:, :D_NOPE] * Q_SCALE).astype(o_ref.dtype)
        o_ref[:, lo + D_NOPE:lo + D_QK] = (_rope_lanes(res[:, D_NOPE:], tab) * Q_SCALE).astype(o_ref.dtype)


def _q_proj(px, q_g, wq, tab, rows_per_seq, tm):
    m = px.shape[0]
    per = rows_per_seq // tm
    return pl.pallas_call(
        _q_kernel,
        grid=(m // tm,),
        in_specs=[pl.BlockSpec((tm, Q_LORA), lambda i: (i, P_Q // Q_LORA)),
                  pl.BlockSpec((1, Q_LORA), lambda i: (0, 0)),
                  pl.BlockSpec((Q_LORA, N_HEADS * D_QK), lambda i: (0, 0)),
                  pl.BlockSpec((tm, 2 * D_ROPE), lambda i: (i % per, 0))],
        out_specs=pl.BlockSpec((tm, N_HEADS * D_QK), lambda i: (i, 0)),
        out_shape=jax.ShapeDtypeStruct((m, N_HEADS * D_QK), BF16),
        compiler_params=_cparams("arbitrary"),
        name="mla_q_proj",
    )(px, q_g, wq, tab)


def _kv_kernel(p_ref, g_ref, wk_ref, wvt_ref, tab_ref, k_ref, vt_ref, kn2_ref):
    p = p_ref[:, :KV_LORA].astype(F32)
    r = lax.rsqrt(jnp.mean(p * p, axis=-1, keepdims=True) + EPS)
    ckv = ((p * r) * g_ref[...]).astype(BF16)
    kr = _rope_lanes(p_ref[:, KV_LORA:KV_LORA + 2 * D_ROPE].astype(F32), tab_ref[...])
    lane = lax.broadcasted_iota(jnp.int32, kr.shape, 1)
    kr = jnp.where(lane < D_ROPE, kr, 0.0).astype(k_ref.dtype)
    kn = jnp.dot(ckv, wk_ref[...], preferred_element_type=F32).astype(k_ref.dtype)
    krf = kr.astype(F32)
    kr2 = jnp.sum(krf * krf, axis=-1, keepdims=True)
    norms = []
    for h in range(N_HEADS):
        kh = kn[:, h * D_NOPE:(h + 1) * D_NOPE]
        k_ref[0, :, h * D_QK:h * D_QK + D_NOPE] = kh
        k_ref[0, :, h * D_QK + D_NOPE:(h + 1) * D_QK] = kr
        khf = kh.astype(F32)
        n2 = jnp.max(jnp.sum(khf * khf, axis=-1, keepdims=True) + kr2, axis=0, keepdims=True)
        norms.append(jnp.broadcast_to(n2, (1, LANES)))
    vt = lax.dot_general(wvt_ref[...], ckv, (((1,), (1,)), ((), ())), preferred_element_type=F32)
    vt_ref[0] = vt.astype(vt_ref.dtype)

    norms = jnp.concatenate(norms, axis=0)

    @pl.when(pl.program_id(1) == 0)
    def _():
        kn2_ref[0] = norms

    @pl.when(pl.program_id(1) > 0)
    def _():
        kn2_ref[0] = jnp.maximum(kn2_ref[0], norms)


def _kv_proj(p, col_blk, kv_g, wk, wvt, tab, bsz, ltok, tm):
    per = ltok // tm
    kw = 2 * KV_LORA
    return pl.pallas_call(
        _kv_kernel,
        grid=(bsz, per),
        in_specs=[pl.BlockSpec((tm, kw), lambda b, t: (b * per + t, col_blk)),
                  pl.BlockSpec((1, KV_LORA), lambda b, t: (0, 0)),
                  pl.BlockSpec((KV_LORA, N_HEADS * D_NOPE), lambda b, t: (0, 0)),
                  pl.BlockSpec((MLA_W, KV_LORA), lambda b, t: (0, 0)),
                  pl.BlockSpec((tm, 2 * D_ROPE), lambda b, t: (t, 0))],
        out_specs=[pl.BlockSpec((1, tm, N_HEADS * D_QK), lambda b, t: (b, t, 0)),
                   pl.BlockSpec((1, MLA_W, tm), lambda b, t: (b, 0, t)),
                   pl.BlockSpec((1, N_HEADS, LANES), lambda b, t: (b, 0, 0))],
        out_shape=[jax.ShapeDtypeStruct((bsz, ltok, N_HEADS * D_QK), BF16),
                   jax.ShapeDtypeStruct((bsz, MLA_W, ltok), BF16),
                   jax.ShapeDtypeStruct((bsz, N_HEADS, LANES), F32)],
        compiler_params=_cparams("arbitrary", "arbitrary"),
        name="mla_kv_proj",
    )(p, kv_g, wk, wvt, tab)


def _attn_kernel(q_ref, *refs, n_sub, n_kv):
    k_refs, vt_refs, kn2_refs = refs[0:3 * n_kv:3], refs[1:3 * n_kv:3], refs[2:3 * n_kv:3]
    o_ref, s_ref = refs[3 * n_kv:]
    rows = q_ref.shape[1] // n_sub
    offs = [0]
    for k_ref in k_refs:
        offs.append(offs[-1] + k_ref.shape[1])

    h = pl.program_id(1)
    k2 = functools.reduce(jnp.maximum, [r[0, pl.ds(h, 1), :] for r in kn2_refs])
    k2 = jnp.max(k2, axis=1, keepdims=True)
    qf = q_ref[0].astype(F32)
    q2 = lax.dot_general(jnp.ones((SUB, D_QK), BF16), (qf * qf).astype(BF16), _NT,
                         preferred_element_type=F32)[0:1]
    bound = jnp.sqrt(q2 * k2) * BOUND_SLACK
    safe = jnp.max(bound) <= SAFE_BOUND

    def finish_out(t, ot, l):
        o_ref[0, t * rows:(t + 1) * rows, :] = (ot / l).T.astype(o_ref.dtype)

    @pl.when(safe)
    def _():
        frows = min(FAST_ROWS, q_ref.shape[1])
        for r0 in range(0, q_ref.shape[1], frows):
            q = q_ref[0, r0:r0 + frows, :]
            b = bound[:, r0:r0 + frows]
            l = jnp.zeros((1, frows), F32)
            ot = jnp.zeros((D_V, frows), F32)
            for k_ref, vt_ref in zip(k_refs, vt_refs):
                pt = jnp.exp2(lax.dot_general(k_ref[0], q, _NT, preferred_element_type=F32) - b)
                l = l + jnp.sum(pt, axis=0, keepdims=True)
                ot = ot + jnp.dot(vt_ref[0], pt.astype(BF16), preferred_element_type=F32)
            o_ref[0, r0:r0 + frows, :] = (ot / l).T.astype(o_ref.dtype)

    @pl.when(jnp.logical_not(safe))
    def _():
        def scores(t):
            q = q_ref[0, t * rows:(t + 1) * rows, :]
            for i, k_ref in enumerate(k_refs):
                s_ref[t % 2, offs[i]:offs[i + 1]] = lax.dot_general(k_ref[0], q, _NT, preferred_element_type=F32)

        def finish(t):
            st = s_ref[t % 2]
            m = jnp.max(st, axis=0, keepdims=True)
            pt = jnp.exp2(st - m)
            l = jnp.sum(pt, axis=0, keepdims=True)
            pt = pt.astype(BF16)
            ot = sum(jnp.dot(vt_ref[0], pt[offs[i]:offs[i + 1]], preferred_element_type=F32)
                     for i, vt_ref in enumerate(vt_refs))
            finish_out(t, ot, l)

        scores(0)
        for t in range(n_sub):
            if t + 1 < n_sub:
                scores(t + 1)
            finish(t)


def _attention(q, kv_sets, tq, n_sub=1):
    bsz, lq, _ = q.shape
    in_specs = [pl.BlockSpec((1, tq, D_QK), lambda b, h, i: (b, i, h))]
    args = [q]
    for k, vt, kn2 in kv_sets:
        lk = k.shape[1]
        in_specs += [pl.BlockSpec((1, lk, D_QK), lambda b, h, i: (b, 0, h)),
                     pl.BlockSpec((1, D_V, lk), lambda b, h, i: (b, h, 0)),
                     pl.BlockSpec((1, N_HEADS, LANES), lambda b, h, i: (b, 0, 0))]
        args += [k, vt, kn2]
    lk_all = sum(kv[0].shape[1] for kv in kv_sets)
    return pl.pallas_call(
        functools.partial(_attn_kernel, n_sub=n_sub, n_kv=len(kv_sets)),
        grid=(bsz, N_HEADS, lq // tq),
        in_specs=in_specs,
        out_specs=pl.BlockSpec((1, tq, D_V), lambda b, h, i: (b, i, h)),
        out_shape=jax.ShapeDtypeStruct((bsz, lq, MLA_W), BF16),
        scratch_shapes=[pltpu.VMEM((2, lk_all, tq // n_sub), F32)],
        compiler_params=_cparams("arbitrary", "arbitrary", "arbitrary"),
        name="mla_attention",
    )(*args)


def _filter_mlp(z_ref, w1_ref, b1_ref, fr_ref, w2_ref, b2_ref):
    hi = lax.Precision.HIGHEST
    fr = fr_ref[...]
    h = jnp.sin(fr * (jnp.dot(z_ref[...], w1_ref[...], precision=hi, preferred_element_type=F32) + b1_ref[...]))
    return jnp.sin(fr * (jnp.dot(h, w2_ref[...], precision=hi, preferred_element_type=F32) + b2_ref[...]))


def _filter_taps(h, z_ref, w3_ref, dl_ref, zero_first):
    taps = jnp.dot(h.astype(BF16), w3_ref[...].astype(BF16), preferred_element_type=F32)
    taps = taps * jnp.exp(-z_ref[:, 0:1] * dl_ref[...])
    if zero_first:
        row = lax.broadcasted_iota(jnp.int32, taps.shape, 0)
        taps = jnp.where(row == 0, 0.0, taps)
    return taps


def _fill_padded(pad_ref, src, n):
    ct = pad_ref.shape[1]
    pad_ref[0:SUB, :] = jnp.zeros((SUB, ct), F32)
    pad_ref[SUB + n:2 * SUB + n, :] = jnp.zeros((SUB, ct), F32)
    pad_ref[SUB:SUB + n, :] = src.astype(F32)


def _short_conv(pad_ref, w_ref, b_ref, r0, rows):
    n = rows + 2 * SUB
    win = pad_ref[r0:r0 + n, :]
    prev = pltpu.roll(win, 1, 0)[SUB:SUB + rows]
    nxt = pltpu.roll(win, n - 1, 0)[SUB:SUB + rows]
    return (b_ref[...] + prev * w_ref[0:1, :] + win[SUB:SUB + rows] * w_ref[1:2, :] + nxt * w_ref[2:3, :])


def _cmul(ur, ui, kr, ki):
    return ur * kr - ui * ki, ur * ki + ui * kr


CONV_CHUNK = 512
MID_UNROLL = 33


def _stage1(u_ref, kf_ref, a_ref):
    n1h, _, ct = u_ref.shape
    for j in range(FFT_N2 // SUB):
        slab = u_ref[:, j * SUB:(j + 1) * SUB, :].reshape(n1h * SUB, ct).astype(BF16)
        r = jnp.dot(kf_ref[...], slab, preferred_element_type=F32)
        a_ref[:, :, j * SUB:(j + 1) * SUB, :] = r.reshape(2, FFT_H1, SUB, ct)


def _stage2(a_ref, rb_ref, i):
    ct = a_ref.shape[-1]
    x = a_ref[:, i].reshape(2 * FFT_N2, ct).astype(BF16)
    return jnp.dot(rb_ref[i], x, preferred_element_type=F32)


def _lfilt_kernel(z_ref, w1_ref, b1_ref, fr_ref, w2_ref, b2_ref, w3f_ref, w3b_ref, dl_ref,
                  kf_ref, rb_ref, o_ref, h_ref, u_ref, a_ref):
    ct = o_ref.shape[-1]
    n1h = u_ref.shape[0]

    @pl.when(pl.program_id(0) == 0)
    def _():
        h_ref[...] = _filter_mlp(z_ref, w1_ref, b1_ref, fr_ref, w2_ref, b2_ref)

    u_ref[:, :, :ct] = _filter_taps(h_ref[...], z_ref, w3f_ref, dl_ref, zero_first=False).reshape(n1h, FFT_N2, ct)
    u_ref[:, :, ct:] = _filter_taps(h_ref[...], z_ref, w3b_ref, dl_ref, zero_first=True).reshape(n1h, FFT_N2, ct)
    _stage1(u_ref, kf_ref, a_ref)

    def body(i, carry):
        s = _stage2(a_ref, rb_ref, i).reshape(2, FFT_N2, 2 * ct)
        o_ref[0, i] = (s[0, :, :ct] + s[0, :, ct:]).astype(o_ref.dtype)
        o_ref[1, i] = (s[1, :, :ct] - s[1, :, ct:]).astype(o_ref.dtype)
        return carry

    lax.fori_loop(0, FFT_H1, body, 0, unroll=MID_UNROLL)


def _long_filter_spectrum(lp, L, ct):
    kron_f, _, rb, _ = _dft_tables()
    z = jnp.asarray(_filter_features(L))
    dl = jnp.asarray(_filter_deltas())
    nct = HY_W // ct
    full = lambda shape: pl.BlockSpec(shape, lambda j: (0,) * len(shape))
    return pl.pallas_call(
        _lfilt_kernel,
        grid=(nct,),
        in_specs=[full((L, FILT_EMB)), full((FILT_EMB, FILT_HIDDEN)), full((1, FILT_HIDDEN)),
                  full((1, FILT_HIDDEN)), full((FILT_HIDDEN, FILT_HIDDEN)), full((1, FILT_HIDDEN)),
                  pl.BlockSpec((FILT_HIDDEN, ct), lambda j: (0, j)),
                  pl.BlockSpec((FILT_HIDDEN, ct), lambda j: (0, j + nct)),
                  pl.BlockSpec((1, ct), lambda j: (0, j)),
                  _const_spec(kron_f.shape), _const_spec(rb.shape)],
        out_specs=pl.BlockSpec((2, FFT_H1, FFT_N2, ct), lambda j: (0, 0, 0, j)),
        out_shape=jax.ShapeDtypeStruct((2, FFT_H1, FFT_N2, HY_W), BF16),
        scratch_shapes=[pltpu.VMEM((L, FILT_HIDDEN), F32),
                        pltpu.VMEM((FFT_N1 // 2, FFT_N2, 2 * ct), F32),
                        pltpu.VMEM((2, FFT_H1, FFT_N2, 2 * ct), F32)],
        compiler_params=_cparams("arbitrary"),
        name="hyena_filter_spectrum",
    )(z, lp["filt_w1"], lp["filt_b1"][None], lp["filt_freq"][None], lp["filt_w2"], lp["filt_b2"][None],
      lp["filt_w3"], lp["filt_w3"], dl, jnp.asarray(kron_f).astype(BF16), jnp.asarray(rb).astype(BF16))


def _lhyena_kernel(x0_ref, x1_ref, v_ref, w0_ref, w1_ref, wv_ref, b0_ref, b1_ref, bv_ref, d_ref,
                   ks_ref, kf_ref, ki_ref, rb_ref, rbt_ref, o_ref, pad_ref, u_ref, a_ref):
    n1h, n2, ct = u_ref.shape
    L = n1h * n2
    nchunk = L // CONV_CHUNK
    cpt = CONV_CHUNK // n2

    _fill_padded(pad_ref, x1_ref[0].reshape(L, ct), L)
    for c in range(nchunk):
        u_ref[c * cpt:(c + 1) * cpt] = _short_conv(pad_ref, w1_ref, b1_ref, c * CONV_CHUNK, CONV_CHUNK).reshape(cpt, n2, ct)
    _fill_padded(pad_ref, v_ref[0].reshape(L, ct), L)
    for c in range(nchunk):
        vv = _short_conv(pad_ref, wv_ref, bv_ref, c * CONV_CHUNK, CONV_CHUNK).reshape(cpt, n2, ct)
        u_ref[c * cpt:(c + 1) * cpt] = u_ref[c * cpt:(c + 1) * cpt] * vv

    _stage1(u_ref, kf_ref, a_ref)

    def body(i, carry):
        uh = _stage2(a_ref, rb_ref, i)
        kr = ks_ref[0, i].astype(F32)
        ki = ks_ref[1, i].astype(F32)
        zr, zi = _cmul(uh[:n2], uh[n2:], kr, ki)
        zz = jnp.concatenate([zr, zi], axis=0).astype(BF16)
        g = jnp.dot(rbt_ref[i], zz, preferred_element_type=F32)
        a_ref[:, i] = g.reshape(2, n2, ct)
        return carry

    lax.fori_loop(0, FFT_H1, body, 0, unroll=MID_UNROLL)

    for j in range(n2 // SUB):
        g = jnp.concatenate([
            a_ref[0, :, j * SUB:(j + 1) * SUB, :].reshape(FFT_H1 * SUB, ct),
            a_ref[1, 1:FFT_H1 - 1, j * SUB:(j + 1) * SUB, :].reshape((FFT_H1 - 2) * SUB, ct)], axis=0).astype(BF16)
        y = jnp.dot(ki_ref[...], g, preferred_element_type=F32).reshape(n1h, SUB, ct)
        o_ref[0, :, j * SUB:(j + 1) * SUB, :] = y + u_ref[:, j * SUB:(j + 1) * SUB, :] * d_ref[...]

    _fill_padded(pad_ref, x0_ref[0].reshape(L, ct), L)
    for c in range(nchunk):
        x0 = _short_conv(pad_ref, w0_ref, b0_ref, c * CONV_CHUNK, CONV_CHUNK).reshape(cpt, n2, ct)
        o_ref[0, c * cpt:(c + 1) * cpt] = o_ref[0, c * cpt:(c + 1) * cpt] * x0


def _long_hyena(px, lp, kspec, bsz, L, ct):
    kron_f, kron_i, rb, rbt = _dft_tables()
    n1h = L // FFT_N2
    px4 = px.reshape(bsz, n1h, FFT_N2, N_PROJ)
    nct = HY_W // ct
    c0 = P_HY // ct
    part = lambda k: pl.BlockSpec((1, n1h, FFT_N2, ct), lambda j, b: (b, 0, 0, c0 + k * nct + j))
    wpart = lambda k: pl.BlockSpec((3, ct), lambda j, b: (0, k * nct + j))
    bpart = lambda k: pl.BlockSpec((1, ct), lambda j, b: (0, k * nct + j))
    cb = lp["conv_b"][None]
    out = pl.pallas_call(
        _lhyena_kernel,
        grid=(nct, bsz),
        in_specs=[part(0), part(1), part(2), wpart(0), wpart(1), wpart(2), bpart(0), bpart(1), bpart(2),
                  pl.BlockSpec((1, ct), lambda j, b: (0, j)),
                  pl.BlockSpec((2, FFT_H1, FFT_N2, ct), lambda j, b: (0, 0, 0, j), pipeline_mode=pl.Buffered(1)),
                  _const_spec(kron_f.shape), _const_spec(kron_i.shape),
                  _const_spec(rb.shape), _const_spec(rbt.shape)],
        out_specs=pl.BlockSpec((1, n1h, FFT_N2, ct), lambda j, b: (b, 0, 0, j)),
        out_shape=jax.ShapeDtypeStruct((bsz, n1h, FFT_N2, HY_W), F32),
        scratch_shapes=[pltpu.VMEM((L + 2 * SUB, ct), F32),
                        pltpu.VMEM((n1h, FFT_N2, ct), F32),
                        pltpu.VMEM((2, FFT_H1, FFT_N2, ct), F32)],
        compiler_params=_cparams("arbitrary", "arbitrary"),
        name="hyena_long_conv",
    )(px4, px4, px4, lp["conv_w"], lp["conv_w"], lp["conv_w"], cb, cb, cb, lp["hy_D"][None], kspec,
      jnp.asarray(kron_f).astype(BF16), jnp.asarray(kron_i).astype(BF16),
      jnp.asarray(rb).astype(BF16), jnp.asarray(rbt).astype(BF16))
    return out.reshape(bsz * L, HY_W)


def _sfilt_kernel(z_ref, w1_ref, b1_ref, fr_ref, w2_ref, b2_ref, w3f_ref, w3b_ref, dl_ref, f_ref, o_ref):
    h = _filter_mlp(z_ref, w1_ref, b1_ref, fr_ref, w2_ref, b2_ref)
    nfp = f_ref.shape[0] // 2
    sf = jnp.dot(f_ref[...], _filter_taps(h, z_ref, w3f_ref, dl_ref, False).astype(BF16), preferred_element_type=F32)
    sb = jnp.dot(f_ref[...], _filter_taps(h, z_ref, w3b_ref, dl_ref, True).astype(BF16), preferred_element_type=F32)
    o_ref[0:nfp, :] = sf[:nfp] + sb[:nfp]
    o_ref[nfp:, :] = sf[nfp:] - sb[nfp:]


def _short_filter_spectrum(lp, L, ct):
    fwd, _ = _small_dft_tables(L)
    z = jnp.asarray(_filter_features(L))
    dl = jnp.asarray(_filter_deltas())
    nct = HY_W // ct
    full = lambda shape: pl.BlockSpec(shape, lambda j: (0,) * len(shape))
    return pl.pallas_call(
        _sfilt_kernel,
        grid=(nct,),
        in_specs=[full((L, FILT_EMB)), full((FILT_EMB, FILT_HIDDEN)), full((1, FILT_HIDDEN)),
                  full((1, FILT_HIDDEN)), full((FILT_HIDDEN, FILT_HIDDEN)), full((1, FILT_HIDDEN)),
                  pl.BlockSpec((FILT_HIDDEN, ct), lambda j: (0, j)),
                  pl.BlockSpec((FILT_HIDDEN, ct), lambda j: (0, j + nct)),
                  pl.BlockSpec((1, ct), lambda j: (0, j)),
                  full(fwd.shape)],
        out_specs=pl.BlockSpec((fwd.shape[0], ct), lambda j: (0, j)),
        out_shape=jax.ShapeDtypeStruct((fwd.shape[0], HY_W), F32),
        compiler_params=_cparams("arbitrary"),
        name="hyena_ctx_filter_spectrum",
    )(z, lp["filt_w1"], lp["filt_b1"][None], lp["filt_freq"][None], lp["filt_w2"], lp["filt_b2"][None],
      lp["filt_w3"], lp["filt_w3"], dl, jnp.asarray(fwd).astype(BF16))


def _shyena_kernel(x0_ref, x1_ref, v_ref, w0_ref, w1_ref, wv_ref, b0_ref, b1_ref, bv_ref, d_ref,
                   ks_ref, f_ref, fi_ref, o_ref, pad_ref):
    L = x0_ref.shape[0]
    nfp = f_ref.shape[0] // 2
    _fill_padded(pad_ref, x1_ref[...], L)
    u = _short_conv(pad_ref, w1_ref, b1_ref, 0, L)
    _fill_padded(pad_ref, v_ref[...], L)
    u = u * _short_conv(pad_ref, wv_ref, bv_ref, 0, L)
    uh = jnp.dot(f_ref[...], u.astype(BF16), preferred_element_type=F32)
    zr, zi = _cmul(uh[:nfp], uh[nfp:], ks_ref[0:nfp, :], ks_ref[nfp:, :])
    zz = jnp.concatenate([zr, zi], axis=0).astype(BF16)
    y = jnp.dot(fi_ref[...], zz, preferred_element_type=F32)
    _fill_padded(pad_ref, x0_ref[...], L)
    o_ref[...] = _short_conv(pad_ref, w0_ref, b0_ref, 0, L) * (y + u * d_ref[...])


def _short_hyena(pc, lp, kspec, bsz, L, ct):
    fwd, inv = _small_dft_tables(L)
    nct = HY_W // ct
    c0 = P_HY // ct
    part = lambda k: pl.BlockSpec((L, ct), lambda j, b: (b, c0 + k * nct + j))
    wpart = lambda k: pl.BlockSpec((3, ct), lambda j, b: (0, k * nct + j))
    bpart = lambda k: pl.BlockSpec((1, ct), lambda j, b: (0, k * nct + j))
    full = lambda shape: pl.BlockSpec(shape, lambda j, b: (0,) * len(shape))
    cb = lp["conv_b"][None]
    return pl.pallas_call(
        _shyena_kernel,
        grid=(nct, bsz),
        in_specs=[part(0), part(1), part(2), wpart(0), wpart(1), wpart(2), bpart(0), bpart(1), bpart(2),
                  pl.BlockSpec((1, ct), lambda j, b: (0, j)),
                  pl.BlockSpec((fwd.shape[0], ct), lambda j, b: (0, j)),
                  full(fwd.shape), full(inv.shape)],
        out_specs=pl.BlockSpec((L, ct), lambda j, b: (b, j)),
        out_shape=jax.ShapeDtypeStruct((bsz * L, HY_W), F32),
        scratch_shapes=[pltpu.VMEM((L + 2 * SUB, ct), F32)],
        compiler_params=_cparams("arbitrary", "arbitrary"),
        name="hyena_ctx_conv",
    )(pc, pc, pc, lp["conv_w"], lp["conv_w"], lp["conv_w"], cb, cb, cb, lp["hy_D"][None], kspec,
      jnp.asarray(fwd).astype(BF16), jnp.asarray(inv).astype(BF16))


def _merge_kernel(o_ref, gm_ref, y_ref, gh_ref, x_ref, gx_ref, gmla_ref, ghy_ref, pg_ref, w_ref, out_ref):
    def normed(t, g_ref, gate_ref):
        r = lax.rsqrt(jnp.mean(t * t, axis=-1, keepdims=True) + EPS)
        gate = gate_ref[...].astype(F32)
        return ((t * r) * g_ref[...] * (gate * jax.nn.sigmoid(gate))).astype(BF16)

    a = normed(o_ref[...].astype(F32), gmla_ref, gm_ref)
    b = normed(y_ref[...], ghy_ref, gh_ref)
    z = (jnp.dot(a, w_ref[:MLA_W, :], preferred_element_type=F32)
         + jnp.dot(b, w_ref[MLA_W:, :], preferred_element_type=F32))
    r = lax.rsqrt(jnp.mean(z * z, axis=-1, keepdims=True) + EPS)
    out_ref[...] = x_ref[...] + gx_ref[0] * ((z * r) * pg_ref[...])


def _merge(o, px, yh, x2d, gx, lp, w_out, rows_per_mod, tm):
    m, d = x2d.shape
    per = rows_per_mod // tm
    return pl.pallas_call(
        _merge_kernel,
        grid=(m // tm,),
        in_specs=[pl.BlockSpec((tm, MLA_W), lambda i: (i, 0)),
                  pl.BlockSpec((tm, MLA_W), lambda i: (i, P_GM // MLA_W)),
                  pl.BlockSpec((tm, HY_W), lambda i: (i, 0)),
                  pl.BlockSpec((tm, HY_W), lambda i: (i, P_GH // HY_W)),
                  pl.BlockSpec((tm, d), lambda i: (i, 0)),
                  pl.BlockSpec((1, 1, d), lambda i: (i // per, 0, 0)),
                  pl.BlockSpec((1, MLA_W), lambda i: (0, 0)),
                  pl.BlockSpec((1, HY_W), lambda i: (0, 0)),
                  pl.BlockSpec((1, d), lambda i: (0, 0)),
                  _const_spec((MLA_W + HY_W, d))],
        out_specs=pl.BlockSpec((tm, d), lambda i: (i, 0)),
        out_shape=jax.ShapeDtypeStruct((m, d), F32),
        compiler_params=_cparams("arbitrary"),
        name="branch_merge_out_proj",
    )(o, px, yh, px, x2d, gx, lp["grp_g_mla"][None], lp["grp_g_hy"][None], lp["post_g"][None], w_out)


def _head_rows(wt_all, layer):
    off_kr = Q_LORA + KV_LORA
    head = wt_all[layer, :BULK_ROW0]
    kr_rot = _rot_half_cols(head[off_kr:].T).T
    pad = jnp.zeros((P_GM - BULK_ROW0 - D_ROPE, head.shape[1]), head.dtype)
    return jnp.concatenate([head, kr_rot, pad], axis=0)


def _prep_weights(lp, wt_all, layer):
    w_head = _head_rows(wt_all, layer)
    wq = lp["w_uq"].reshape(Q_LORA, N_HEADS, D_NOPE + D_ROPE)
    wq_r = wq[..., D_NOPE:]
    wq_cat = jnp.concatenate([wq[..., :D_NOPE], wq_r, _rot_half_cols(wq_r)], axis=-1)
    wq_cat = wq_cat.reshape(Q_LORA, N_HEADS * D_QK).astype(BF16)
    wkv = lp["w_ukv"].reshape(KV_LORA, N_HEADS, D_NOPE + D_V)
    wk = wkv[..., :D_NOPE].reshape(KV_LORA, N_HEADS * D_NOPE).astype(BF16)
    wvt = wkv[..., D_NOPE:].reshape(KV_LORA, MLA_W).T.astype(BF16)
    return w_head, wq_cat, wk, wvt, lp["w_out"].astype(BF16)


def _layer(x2d, c2d, mod, lp, wt_all, layer, bsz, L, Lc, update_ctx):
    d = D_MODEL
    w_head, wq_cat, wk, wvt, w_out = _prep_weights(lp, wt_all, layer)
    sh, sc, gt = mod[:, :d], mod[:, d:2 * d], mod[:, 2 * d:]
    sh_x, sc_x, g_x = (t[:bsz, None, :] for t in (sh, sc, gt))
    sh_c, sc_c, g_c = (t[bsz:bsz + 1, None, :] for t in (sh, sc, gt))
    pre_g = lp["pre_g"][None]

    px = _input_proj(x2d, sc_x, sh_x, pre_g, w_head, wt_all, layer, rows_per_mod=L, tm=TM_WIN)
    kv_blk = P_KV // (2 * KV_LORA)
    ctx_tm = min(TM_WIN, bsz * Lc)
    if update_ctx:
        pc = _input_proj(c2d, sc_c, sh_c, pre_g, w_head, wt_all, layer, rows_per_mod=bsz * Lc, tm=ctx_tm)
        pc_kv, pc_blk = pc, kv_blk
    else:
        pc_kv = _input_proj_kv(c2d, sc_c, sh_c, pre_g, w_head, tm=ctx_tm)
        pc_blk = 0

    tab_x = jnp.asarray(_rope_table(L))
    tab_c = jnp.asarray(_identity_rope_table(Lc))
    kv_g = lp["kv_norm_g"][None]
    kv_x = _kv_proj(px, kv_blk, kv_g, wk, wvt, tab_x, bsz, L, tm=TM_QKV)
    kv_c = _kv_proj(pc_kv, pc_blk, kv_g, wk, wvt, tab_c, bsz, Lc, tm=Lc)
    q_x = _q_proj(px, lp["q_norm_g"][None], wq_cat, tab_x, rows_per_seq=L, tm=TM_QKV)
    o_x = _attention(q_x.reshape(bsz, L, -1), [kv_x, kv_c], tq=TQ_ATTN, n_sub=ATTN_SUB).reshape(bsz * L, MLA_W)

    kspec = _long_filter_spectrum(lp, L, ct=CT_HYENA)
    y_x = _long_hyena(px, lp, kspec, bsz, L, ct=CT_HYENA)
    x_new = _merge(o_x, px, y_x, x2d, g_x, lp, w_out, rows_per_mod=L, tm=TM_MERGE)

    c_new = c2d
    if update_ctx:
        q_c = _q_proj(pc, lp["q_norm_g"][None], wq_cat, tab_c, rows_per_seq=Lc, tm=Lc)
        o_c = _attention(q_c.reshape(bsz, Lc, -1), [kv_c], tq=Lc).reshape(bsz * Lc, MLA_W)
        kspec_c = _short_filter_spectrum(lp, Lc, ct=CT_HYENA)
        y_c = _short_hyena(pc, lp, kspec_c, bsz, Lc, ct=CT_HYENA)
        c_new = _merge(o_c, pc, y_c, c2d, g_c, lp, w_out, rows_per_mod=bsz * Lc, tm=min(TM_MERGE, bsz * Lc))
    return x_new, c_new


def kernel(x, c, ctx, c_ctx, ada_w, ada_b, pre_g, w_in, q_norm_g, w_uq, kv_norm_g, w_ukv, conv_w, conv_b,
           filt_w1, filt_b1, filt_freq, filt_w2, filt_b2, filt_w3, hy_D, grp_g_mla, grp_g_hy, w_out, post_g):
    bsz, L, d = x.shape
    Lc = ctx.shape[1]
    depth = ada_w.shape[0]
    assert d == D_MODEL and L == (FFT_N1 // 2) * FFT_N2 and bsz + 1 <= SUB
    cvec = jnp.concatenate([c, c_ctx[None], jnp.zeros((SUB - bsz - 1, d), F32)], axis=0)
    mod = _modulation(cvec, ada_w, ada_b)
    wt_all = jnp.swapaxes(w_in, 1, 2).astype(BF16)
    params = dict(pre_g=pre_g, q_norm_g=q_norm_g, w_uq=w_uq, kv_norm_g=kv_norm_g, w_ukv=w_ukv,
                  conv_w=conv_w, conv_b=conv_b, filt_w1=filt_w1, filt_b1=filt_b1, filt_freq=filt_freq,
                  filt_w2=filt_w2, filt_b2=filt_b2, filt_w3=filt_w3, hy_D=hy_D, grp_g_mla=grp_g_mla,
                  grp_g_hy=grp_g_hy, w_out=w_out, post_g=post_g)
    x2d = x.reshape(bsz * L, d)
    c2d = ctx.reshape(bsz * Lc, d)
    for l in range(depth):
        lp = {k: v[l] for k, v in params.items()}
        x2d, c2d = _layer(x2d, c2d, mod[l], lp, wt_all, l, bsz, L, Lc, update_ctx=(l < depth - 1))
    return x2d.reshape(bsz, L, d)
```
